```python
import math
import jax, jax.numpy as jnp
from jax import lax
import numpy as np

D_MODEL = 1024
BATCH = 4
SEQ = 8192
DEPTH = 4

N_A_LAYERS = DEPTH // 2
N_B_LAYERS = DEPTH - N_A_LAYERS

GDN_HEADS = 8
GDN_DK = 128
GDN_DV = 128
GDN_CONV = 4
GDN_CHUNK = 64
GDN_QK = GDN_HEADS * GDN_DK
GDN_V = GDN_HEADS * GDN_DV
GDN_PROJ = 2 * GDN_QK + 2 * GDN_V + 2 * GDN_HEADS

DIFF_HEADS = 8
DIFF_DH = 64
DIFF_DV = 2 * DIFF_DH
DIFF_Q = DIFF_HEADS * 2 * DIFF_DH
DIFF_K = DIFF_HEADS * 2 * DIFF_DH
DIFF_V = DIFF_HEADS * DIFF_DV
Q_BLOCK = 128

FF = 2816
FF_CONV = 3

EPS = 1e-6

kernel_name = "yoco_gdn_diffattn_convffn"


def rms_norm(x, g):
    xf = x.astype(jnp.float32)
    y = xf * lax.rsqrt(jnp.mean(xf * xf, -1, keepdims=True) + EPS)
    return (y * g.astype(jnp.float32)).astype(x.dtype)


def l2_norm(x):
    xf = x.astype(jnp.float32)
    return xf * lax.rsqrt(jnp.sum(xf * xf, -1, keepdims=True) + EPS)


def causal_dwconv(x, w):
    K, C = w.shape
    return lax.conv_general_dilated(
        x, w[:, None, :].astype(x.dtype), window_strides=(1,), padding=[(K - 1, 0)],
        dimension_numbers=('NWC', 'WIO', 'NWC'), feature_group_count=C)


def chunk_gated_delta(q, k, v, g, beta):
    f32 = jnp.float32
    Bn, S, H, dk = q.shape
    dv = v.shape[-1]
    C = GDN_CHUNK
    N = S // C

    def chunk(t):
        t = t.astype(f32).reshape((Bn, N, C, H) + t.shape[3:])
        return jnp.moveaxis(t, 3, 1)

    q, k, v, g, beta = chunk(q), chunk(k), chunk(v), chunk(g), chunk(beta)
    gc = jnp.cumsum(g, -1)
    causal = jnp.tril(jnp.ones((C, C), bool))
    strict = jnp.tril(jnp.ones((C, C), bool), -1)
    decay = jnp.exp(jnp.where(causal, gc[..., :, None] - gc[..., None, :], -jnp.inf))

    kk = jnp.einsum('bhnid,bhnjd->bhnij', k, k)
    A = jnp.where(strict, kk * decay * beta[..., :, None], 0.0)
    T = jnp.eye(C, dtype=f32) + A
    rhs = jnp.concatenate([v * beta[..., None], k * (beta * jnp.exp(gc))[..., None]], -1)
    sol = lax.linalg.triangular_solve(T, rhs, left_side=True, lower=True)
    u, w = sol[..., :dv], sol[..., dv:]

    attn = jnp.einsum('bhnid,bhnjd->bhnij', q, k) * decay
    q_dec = q * jnp.exp(gc)[..., None]
    k_dec = k * jnp.exp(gc[..., -1:] - gc)[..., None]
    g_last = jnp.exp(gc[..., -1])

    def step(state, inp):
        u_n, w_n, attn_n, qd_n, kd_n, gl_n = inp
        U = u_n - jnp.einsum('bhcd,bhde->bhce', w_n, state)
        o = jnp.einsum('bhcd,bhde->bhce', qd_n, state) + jnp.einsum('bhij,bhje->bhie', attn_n, U)
        state = state * gl_n[..., None, None] + jnp.einsum('bhcd,bhce->bhde', kd_n, U)
        return state, o

    xs = tuple(jnp.moveaxis(t, 2, 0) for t in (u, w, attn, q_dec, k_dec, g_last))
    s0 = jnp.zeros((Bn, H, dk, dv), f32)
    _, o = lax.scan(step, s0, xs)
    return o.transpose(1, 0, 3, 2, 4).reshape(Bn, S, H, dv)


def gated_deltanet(h, w_in, conv_w, a_log, dt_bias, out_norm, w_out):
    Bn, S, _ = h.shape
    p = h @ w_in
    qkv, z, a, b = jnp.split(p, [2 * GDN_QK + GDN_V, 2 * GDN_QK + 2 * GDN_V,
                                 2 * GDN_QK + 2 * GDN_V + GDN_HEADS], -1)
    qkv = jax.nn.silu(causal_dwconv(qkv, conv_w))
    q, k, v = jnp.split(qkv, [GDN_QK, 2 * GDN_QK], -1)
    q = l2_norm(q.reshape(Bn, S, GDN_HEADS, GDN_DK)) * (GDN_DK ** -0.5)
    k = l2_norm(k.reshape(Bn, S, GDN_HEADS, GDN_DK))
    v = v.reshape(Bn, S, GDN_HEADS, GDN_DV)
    g = -jnp.exp(a_log.astype(jnp.float32)) * jax.nn.softplus(a.astype(jnp.float32) + dt_bias.astype(jnp.float32))
    beta = jax.nn.sigmoid(b.astype(jnp.float32))
    o = chunk_gated_delta(q, k, v, g, beta)
    o = rms_norm(o, out_norm) * jax.nn.silu(z.reshape(Bn, S, GDN_HEADS, GDN_DV).astype(jnp.float32))
    return o.reshape(Bn, S, GDN_V).astype(h.dtype) @ w_out


def shared_kv(x, kv_norm, kv_w, k_norm):
    Bn, S, _ = x.shape
    kv = rms_norm(x, kv_norm) @ kv_w
    k, v = jnp.split(kv, [DIFF_K], -1)
    k = rms_norm(k.reshape(Bn, S, DIFF_HEADS, 2, DIFF_DH), k_norm)
    k = k.transpose(0, 2, 3, 1, 4)
    v = v.reshape(Bn, S, DIFF_HEADS, DIFF_DV).transpose(0, 2, 1, 3)
    return k, v


def diff_attention(h, k, v, w_q, q_norm, lam, subln, w_out, lambda_init):
    Bn, S, _ = h.shape
    q = (h @ w_q).reshape(Bn, S, DIFF_HEADS, 2, DIFF_DH)
    q = rms_norm(q, q_norm) * (DIFF_DH ** -0.5)
    lf = lam.astype(jnp.float32)
    lambda_full = jnp.exp(jnp.sum(lf[0] * lf[1])) - jnp.exp(jnp.sum(lf[2] * lf[3])) + lambda_init
    nblk = S // Q_BLOCK
    qb = q.reshape(Bn, nblk, Q_BLOCK, DIFF_HEADS, 2, DIFF_DH).transpose(1, 0, 3, 4, 2, 5)
    key_pos = jnp.arange(S)

    def block(args):
        q_blk, i = args
        s = jnp.einsum('bhcqd,bhckd->bhcqk', q_blk, k).astype(jnp.float32)
        q_pos = i * Q_BLOCK + jnp.arange(Q_BLOCK)
        s = jnp.where(key_pos[None, :] <= q_pos[:, None], s, -jnp.inf)
        p = jax.nn.softmax(s, -1)
        a = p[:, :, 0] - lambda_full * p[:, :, 1]
        return jnp.einsum('bhqk,bhkd->bhqd', a.astype(v.dtype), v)

    o = lax.map(block, (qb, jnp.arange(nblk)))
    o = o.transpose(1, 0, 3, 2, 4).reshape(Bn, S, DIFF_HEADS, DIFF_DV)
    o = rms_norm(o, subln) * (1.0 - lambda_init)
    return o.reshape(Bn, S, DIFF_V) @ w_out


def conv_ffn(h, w_up, conv_w, conv_b, w_down):
    u = causal_dwconv(h @ w_up, conv_w) + conv_b
    gate, up = jnp.split(u, 2, -1)
    return (jax.nn.silu(gate) * up) @ w_down


def setup_inputs(seed: int = 0) -> dict:
    key = jax.random.key(seed)
    ks = jax.random.split(key, 24)
    nrm = lambda k, shape, s: jax.random.normal(k, shape, jnp.float32) * s
    gain = lambda k, shape: 1.0 + 0.01 * jax.random.normal(k, shape, jnp.float32)
    out_scale = (2 * DEPTH) ** -0.5
    dt = jnp.exp(jax.random.uniform(ks[5], (N_A_LAYERS, GDN_HEADS), jnp.float32,
                                    math.log(1e-3), math.log(1e-1)))
    return {
        "x": nrm(ks[0], (BATCH, SEQ, D_MODEL), 1.0),
        "a_norm": gain(ks[1], (N_A_LAYERS, D_MODEL)),
        "a_w_in": nrm(ks[2], (N_A_LAYERS, D_MODEL, GDN_PROJ), D_MODEL ** -0.5),
        "a_conv_w": nrm(ks[3], (N_A_LAYERS, GDN_CONV, 2 * GDN_QK + GDN_V), GDN_CONV ** -0.5),
        "a_log": jnp.log(jax.random.uniform(ks[4], (N_A_LAYERS, GDN_HEADS), jnp.float32, 1.0, 16.0)),
        "a_dt_bias": dt + jnp.log(-jnp.expm1(-dt)),
        "a_out_norm": gain(ks[6], (N_A_LAYERS, GDN_DV)),
        "a_w_out": nrm(ks[7], (N_A_LAYERS, GDN_V, D_MODEL), GDN_V ** -0.5 * out_scale),
        "kv_norm": gain(ks[8], (D_MODEL,)),
        "kv_w": nrm(ks[9], (D_MODEL, DIFF_K + DIFF_V), D_MODEL ** -0.5),
        "k_norm": gain(ks[10], (DIFF_DH,)),
        "b_norm": gain(ks[11], (N_B_LAYERS, D_MODEL)),
        "b_w_q": nrm(ks[12], (N_B_LAYERS, D_MODEL, DIFF_Q), D_MODEL ** -0.5),
        "b_q_norm": gain(ks[13], (N_B_LAYERS, DIFF_DH)),
        "b_lambda": nrm(ks[14], (N_B_LAYERS, 4, DIFF_DH), 0.1),
        "b_subln": gain(ks[15], (N_B_LAYERS, DIFF_DV)),
        "b_w_out": nrm(ks[16], (N_B_LAYERS, DIFF_V, D_MODEL), DIFF_V ** -0.5 * out_scale),
        "f_norm": gain(ks[17], (DEPTH, D_MODEL)),
        "f_w_up": nrm(ks[18], (DEPTH, D_MODEL, 2 * FF), D_MODEL ** -0.5),
        "f_conv_w": nrm(ks[19], (DEPTH, FF_CONV, 2 * FF), FF_CONV ** -0.5),
        "f_conv_b": nrm(ks[20], (DEPTH, 2 * FF), 0.01),
        "f_w_down": nrm(ks[21], (DEPTH, FF, D_MODEL), FF ** -0.5 * out_scale),
    }


def reference(x, a_norm, a_w_in, a_conv_w, a_log, a_dt_bias, a_out_norm, a_w_out,
              kv_norm, kv_w, k_norm, b_norm, b_w_q, b_q_norm, b_lambda, b_subln, b_w_out,
              f_norm, f_w_up, f_conv_w, f_conv_b, f_w_down):
    k_sh, v_sh = None, None
    for i in range(DEPTH):
        if i < N_A_LAYERS:
            j = i
            x = x + gated_deltanet(rms_norm(x, a_norm[j]), a_w_in[j], a_conv_w[j], a_log[j],
                                   a_dt_bias[j], a_out_norm[j], a_w_out[j])
        else:
            if i == N_A_LAYERS:
                k_sh, v_sh = shared_kv(x, kv_norm, kv_w, k_norm)
            j = i - N_A_LAYERS
            lambda_init = 0.8 - 0.6 * math.exp(-0.3 * i)
            x = x + diff_attention(rms_norm(x, b_norm[j]), k_sh, v_sh, b_w_q[j], b_q_norm[j],
                                   b_lambda[j], b_subln[j], b_w_out[j], lambda_init)
        x = x + conv_ffn(rms_norm(x, f_norm[i]), f_w_up[i], f_conv_w[i], f_conv_b[i], f_w_down[i])
    return x
```

```python
import functools
import math

import jax
import jax.numpy as jnp
from jax import lax
from jax.experimental import pallas as pl
from jax.experimental.pallas import tpu as pltpu

F32 = jnp.float32
BF16 = jnp.bfloat16

EPS = 1e-6
LANES = 128
HALO = 16
GDN_HEADS = 8
GDN_DK = 128
GDN_CHUNK = 64
GDN_ROWS = 256
DIFF_HEADS = 8
DIFF_DH = 64
FF_COLS = 256
VMEM_LIMIT = 56 * 1024 * 1024


def _params(*sem):
    return pltpu.CompilerParams(dimension_semantics=sem, vmem_limit_bytes=VMEM_LIMIT)


def _resident(shape):
    zeros = (0,) * len(shape)
    return pl.BlockSpec(shape, lambda *_: zeros, pipeline_mode=pl.Buffered(1))


def _rms(x, g):
    ms = jnp.mean(x * x, axis=-1, keepdims=True)
    return x * lax.rsqrt(ms + EPS) * g


def _sigmoid(x):
    return 1.0 / (1.0 + jnp.exp(-x))


def _silu(x):
    return x * _sigmoid(x)


def _dot(a, b):
    return jnp.dot(a.astype(BF16), b.astype(BF16), preferred_element_type=F32)


def _dot_nt(a, b):
    return lax.dot_general(a.astype(BF16), b.astype(BF16), (((1,), (1,)), ((), ())),
                           preferred_element_type=F32)


def _dot_tn(a, b):
    return lax.dot_general(a.astype(BF16), b.astype(BF16), (((0,), (0,)), ((), ())),
                           preferred_element_type=F32)


def _normed_with_halo(x_ref, xh_ref, g_ref):
    g = g_ref[...]
    h = _rms(x_ref[0], g).astype(BF16)
    keep = (pl.program_id(1) > 0).astype(F32)
    hh = (_rms(xh_ref[0], g) * keep).astype(BF16)
    return h, jnp.concatenate([hh, h], axis=0)


def _causal_taps(u, cw, taps):
    y = u[HALO:] * cw[taps - 1:taps]
    for d in range(1, taps):
        y = y + pltpu.roll(u, d, 0)[HALO:] * cw[taps - 1 - d:taps - d]
    return y


def _halo_specs(tm, d):
    tile = pl.BlockSpec((1, tm, d), lambda b, s: (b, s, 0))
    halo = pl.BlockSpec((1, HALO, d), lambda b, s: (b, jnp.maximum(s * (tm // HALO) - 1, 0), 0))
    return tile, halo


def _ffn_kernel(x_ref, xh_ref, g_ref, wg_ref, wu_ref, cwg_ref, cwu_ref, cbg_ref, cbu_ref, wd_ref, o_ref,
                *, n_slabs, taps):
    _, hext = _normed_with_halo(x_ref, xh_ref, g_ref)
    acc = None
    for c in range(n_slabs):
        gate = _causal_taps(_dot(hext, wg_ref[c]), cwg_ref[c], taps) + cbg_ref[c]
        up = _causal_taps(_dot(hext, wu_ref[c]), cwu_ref[c], taps) + cbu_ref[c]
        d = _dot(_silu(gate) * up, wd_ref[c])
        acc = d if acc is None else acc + d
    o_ref[0] = x_ref[0] + acc


def _conv_ffn(x, g, w_up, conv_w, conv_b, w_down, *, tm=512):
    B, S, D = x.shape
    ff = w_down.shape[0]
    taps = conv_w.shape[0]
    n_slabs = ff // FF_COLS
    assert n_slabs * FF_COLS == ff and S % tm == 0

    def slabs(a):
        a = a.reshape(a.shape[0], 2, n_slabs, FF_COLS).transpose(1, 2, 0, 3)
        return a[0], a[1]

    wg, wu = slabs(w_up.astype(BF16))
    cwg, cwu = slabs(conv_w)
    cbg, cbu = slabs(conv_b[None, :])
    wd = w_down.astype(BF16).reshape(n_slabs, FF_COLS, D)
    tile, halo = _halo_specs(tm, D)
    return pl.pallas_call(
        functools.partial(_ffn_kernel, n_slabs=n_slabs, taps=taps),
        grid=(B, S // tm),
        in_specs=[tile, halo, _resident((1, D)),
                  _resident(wg.shape), _resident(wu.shape), _resident(cwg.shape), _resident(cwu.shape),
                  _resident(cbg.shape), _resident(cbu.shape), _resident(wd.shape)],
        out_specs=tile,
        out_shape=jax.ShapeDtypeStruct(x.shape, F32),
        compiler_params=_params("parallel", "parallel"),
        name="conv_ffn",
    )(x, x, g[None, :], wg, wu, cwg, cwu, cbg, cbu, wd)


def _chunk_cumsum(g):
    row = lax.broadcasted_iota(jnp.int32, g.shape, 0) & (GDN_CHUNK - 1)
    step = 1
    while step < GDN_CHUNK:
        g = g + jnp.where(row >= step, pltpu.roll(g, step, 0), 0.0)
        step *= 2
    return g


def _gdn_in_kernel(x_ref, xh_ref, g_ref, wqkv_ref, cw_ref, wz_ref, wab_ref, alog_ref, dtb_ref,
                   q_ref, k_ref, v_ref, z_ref, gcol_ref, grow_ref, *, taps):
    h, hext = _normed_with_halo(x_ref, xh_ref, g_ref)
    qk_cols = GDN_HEADS * GDN_DK
    outs = (q_ref, k_ref, v_ref)
    for c in range(wqkv_ref.shape[1] // 256):
        cols = slice(c * 256, (c + 1) * 256)
        y = _silu(_causal_taps(_dot(hext, wqkv_ref[:, cols]), cw_ref[:, cols], taps))
        for half in range(2):
            col = c * 256 + half * LANES
            yy = y[:, half * LANES:(half + 1) * LANES]
            which, off = divmod(col, qk_cols)
            if which < 2:
                yy = yy * lax.rsqrt(jnp.sum(yy * yy, axis=-1, keepdims=True) + EPS)
            if which == 0:
                yy = yy * (GDN_DK ** -0.5)
            outs[which][0, :, off:off + LANES] = yy.astype(outs[which].dtype)
    for c in range(wz_ref.shape[1] // 256):
        cols = slice(c * 256, (c + 1) * 256)
        z_ref[0, :, cols] = _dot(h, wz_ref[:, cols]).astype(z_ref.dtype)
    ab = _dot(h, wab_ref[...])
    t = ab + dtb_ref[...]
    softplus = jnp.maximum(t, 0.0) + jnp.log1p(jnp.exp(-jnp.abs(t)))
    cum = _chunk_cumsum(-jnp.exp(alog_ref[...]) * softplus)
    lane = lax.broadcasted_iota(jnp.int32, ab.shape, 1)
    gcol_ref[0] = jnp.where(lane < GDN_HEADS, cum, _sigmoid(ab))
    grow_ref[0] = cum.T[:GDN_HEADS, :]


def _gdn_in(x, g, w_in, conv_w, a_log, dt_bias, *, tm=512, act_dtype=BF16):
    B, S, D = x.shape
    H = GDN_HEADS
    qk = H * GDN_DK
    nqkv = conv_w.shape[1]
    nz = nqkv - 2 * qk
    wqkv = w_in[:, :nqkv].astype(BF16)
    wz = w_in[:, nqkv:nqkv + nz].astype(BF16)
    wab = jnp.pad(w_in[:, nqkv + nz:], ((0, 0), (0, LANES - 2 * H))).astype(BF16)
    alog = jnp.pad(a_log, (0, LANES - H))[None, :]
    dtb = jnp.pad(dt_bias, (0, LANES - H))[None, :]
    tile, halo = _halo_specs(tm, D)
    act = lambda n: jax.ShapeDtypeStruct((B, S, n), act_dtype)
    act_spec = lambda n: pl.BlockSpec((1, tm, n), lambda b, s: (b, s, 0))
    return pl.pallas_call(
        functools.partial(_gdn_in_kernel, taps=conv_w.shape[0]),
        grid=(B, S // tm),
        in_specs=[tile, halo, _resident((1, D)), _resident(wqkv.shape), _resident(conv_w.shape),
                  _resident(wz.shape), _resident(wab.shape), _resident(alog.shape), _resident(dtb.shape)],
        out_specs=[act_spec(qk), act_spec(qk), act_spec(nz), act_spec(nz), act_spec(LANES),
                   pl.BlockSpec((1, H, tm), lambda b, s: (b, 0, s))],
        out_shape=[act(qk), act(qk), act(nz), act(nz),
                   jax.ShapeDtypeStruct((B, S, LANES), F32), jax.ShapeDtypeStruct((B, H, S), F32)],
        compiler_params=_params("parallel", "parallel"),
        name="gdn_in",
    )(x, x, g[None, :], wqkv, conv_w, wz, wab, alog, dtb)


def _unit_lower_inverse(a, ri, ci):
    in16 = (ri >> 4) == (ci >> 4)
    in32 = (ri >> 5) == (ci >> 5)
    d = jnp.where(in16, a, 0.0)
    d2 = _dot(d, d)
    d4 = _dot(d2, d2)
    d8 = _dot(d4, d4)
    x = jnp.where(ri == ci, 1.0, 0.0) - d
    x = x + _dot(x, d2)
    x = x + _dot(x, d4)
    x = x + _dot(x, d8)
    a32 = jnp.where(in32, a, 0.0)
    x = x - _dot(x, _dot(a32 - d, x))
    x = x - _dot(x, _dot(a - a32, x))
    return x


def _gdn_chunk_kernel(q_ref, k_ref, v_ref, z_ref, gcol_ref, grow_ref, x_ref, onorm_ref, wout_ref,
                      o_ref, state_ref, acc_ref, *, heads_per_step):
    s = pl.program_id(1)
    hg = pl.program_id(2)
    R, C = GDN_ROWS, GDN_CHUNK
    ri = lax.broadcasted_iota(jnp.int32, (R, R), 0)
    ci = lax.broadcasted_iota(jnp.int32, (R, R), 1)
    same = (ri >> 6) == (ci >> 6)
    causal = same & (ri >= ci)
    strict = same & (ri > ci)
    gcol = gcol_ref[0]
    lane = lax.broadcasted_iota(jnp.int32, gcol.shape, 1)

    total = None
    for j in range(heads_per_step):
        h = hg * heads_per_step + j
        lanes = slice(j * LANES, (j + 1) * LANES)
        q = q_ref[0, :, lanes].astype(F32)
        k = k_ref[0, :, lanes].astype(F32)
        v = v_ref[0, :, lanes].astype(F32)
        gc = jnp.sum(jnp.where(lane == h, gcol, 0.0), axis=1, keepdims=True)
        beta = jnp.sum(jnp.where(lane == h + GDN_HEADS, gcol, 0.0), axis=1, keepdims=True)
        gc_row = grow_ref[0, pl.ds(h, 1), :]
        decay = jnp.exp(jnp.where(causal, gc - gc_row, -jnp.inf))
        kb = k.astype(BF16)
        a = jnp.where(strict, _dot_nt(kb, kb) * decay * beta, 0.0)
        tinv = _unit_lower_inverse(a, ri, ci)
        attn = _dot_nt(q, kb) * decay
        egc = jnp.exp(gc)
        g_last = jnp.concatenate(
            [jnp.broadcast_to(gc[c * C + C - 1:(c + 1) * C, :], (C, 1)) for c in range(R // C)], axis=0)
        sol = _dot(tinv, jnp.concatenate([v * beta, k * (beta * egc)], axis=1))
        u, w = sol[:, :LANES], sol[:, LANES:]
        q_dec = q * egc
        k_dec = k * jnp.exp(g_last - gc)

        @pl.when(s == 0)
        def _():
            state_ref[h] = jnp.zeros(state_ref.shape[1:], F32)

        state = state_ref[h]
        corrected, from_state = [], []
        for c in range(R // C):
            rows = slice(c * C, (c + 1) * C)
            r = _dot(jnp.concatenate([w[rows], q_dec[rows]], axis=0), state)
            uc = u[rows] - r[:C]
            corrected.append(uc)
            from_state.append(r[C:])
            state = state * jnp.exp(g_last[c * C:c * C + 1, :]) + _dot_tn(k_dec[rows], uc)
        state_ref[h] = state
        o = jnp.concatenate(from_state, axis=0) + _dot(attn, jnp.concatenate(corrected, axis=0))
        o = _rms(o, onorm_ref[...]) * _silu(z_ref[0, :, lanes].astype(F32))
        contrib = _dot(o, wout_ref[h])
        total = contrib if total is None else total + contrib

    @pl.when(hg == 0)
    def _():
        acc_ref[...] = total

    @pl.when(hg > 0)
    def _():
        acc_ref[...] += total

    @pl.when(hg == pl.num_programs(2) - 1)
    def _():
        o_ref[0] = x_ref[0] + acc_ref[...]


def _gdn_chunk(x, q, k, v, z, gcol, grow, out_norm, w_out, *, heads_per_step=1):
    B, S, D = x.shape
    H, R = GDN_HEADS, GDN_ROWS
    dv = w_out.shape[0] // H
    hw = heads_per_step * LANES
    wout = w_out.astype(BF16).reshape(H, dv, D)
    head = pl.BlockSpec((1, R, hw), lambda b, s, h: (b, s, h))
    full = pl.BlockSpec((1, R, D), lambda b, s, h: (b, s, 0))
    return pl.pallas_call(
        functools.partial(_gdn_chunk_kernel, heads_per_step=heads_per_step),
        grid=(B, S // R, H // heads_per_step),
        in_specs=[head, head, head, head,
                  pl.BlockSpec((1, R, LANES), lambda b, s, h: (b, s, 0)),
                  pl.BlockSpec((1, H, R), lambda b, s, h: (b, 0, s)),
                  full, _resident((1, dv)), _resident(wout.shape)],
        out_specs=full,
        out_shape=jax.ShapeDtypeStruct(x.shape, F32),
        scratch_shapes=[pltpu.VMEM((H, GDN_DK, dv), F32), pltpu.VMEM((R, D), F32)],
        compiler_params=_params("arbitrary", "arbitrary", "arbitrary"),
        name="gdn_chunk",
    )(q, k, v, z, gcol, grow, x, out_norm[None, :], wout)


def _proj_kernel(x_ref, g_ref, w_ref, gain_ref, *o_refs, n_normed):
    h = _rms(x_ref[0], g_ref[...]).astype(BF16)
    width = o_refs[0].shape[2]
    for c in range(w_ref.shape[1] // 256):
        y = _dot(h, w_ref[:, c * 256:(c + 1) * 256])
        for half in range(2):
            col = c * 256 + half * LANES
            yy = y[:, half * LANES:(half + 1) * LANES]
            if col < n_normed:
                lane = lax.broadcasted_iota(jnp.int32, yy.shape, 1)
                low = lane < DIFF_DH
                sq = yy * yy
                ms = jnp.where(low,
                               jnp.sum(jnp.where(low, sq, 0.0), axis=-1, keepdims=True),
                               jnp.sum(jnp.where(low, 0.0, sq), axis=-1, keepdims=True)) * (1.0 / DIFF_DH)
                yy = yy * lax.rsqrt(ms + EPS) * gain_ref[:, col:col + LANES]
            which, off = divmod(col, width)
            o_refs[which][0, :, off:off + LANES] = yy.astype(o_refs[which].dtype)


def _proj(x, g, w, gain, n_out, *, tm=512):
    B, S, D = x.shape
    n = w.shape[1]
    width = n // n_out
    tile = pl.BlockSpec((1, tm, D), lambda b, s: (b, s, 0))
    return pl.pallas_call(
        functools.partial(_proj_kernel, n_normed=gain.shape[0]),
        grid=(B, S // tm),
        in_specs=[tile, _resident((1, D)), _resident(w.shape), _resident((1, gain.shape[0]))],
        out_specs=[pl.BlockSpec((1, tm, width), lambda b, s: (b, s, 0))] * n_out,
        out_shape=[jax.ShapeDtypeStruct((B, S, width), BF16)] * n_out,
        compiler_params=_params("parallel", "parallel"),
        name="norm_proj",
    )(x, g[None, :], w.astype(BF16), gain[None, :])


def _attn_kernel(lam_ref, q_ref, k_ref, v_ref, sub_ref, o_ref, m_ref, l_ref, acc_ref, *, t, lam_init):
    i = pl.program_id(2)
    q = q_ref[0]
    lane = lax.broadcasted_iota(jnp.int32, q.shape, 1)
    qs = (jnp.where(lane < DIFF_DH, q, jnp.zeros_like(q)), jnp.where(lane < DIFF_DH, jnp.zeros_like(q), q))
    m_ref[...] = jnp.full(m_ref.shape, -jnp.inf, F32)
    l_ref[...] = jnp.zeros(l_ref.shape, F32)
    acc_ref[...] = jnp.zeros(acc_ref.shape, F32)

    def block(j, masked):
        kb = k_ref[0, pl.ds(pl.multiple_of(j * t, t), t), :]
        vb = v_ref[0, pl.ds(pl.multiple_of(j * t, t), t), :]
        for c in range(2):
            sc = _dot_nt(qs[c], kb)
            if masked:
                ri = lax.broadcasted_iota(jnp.int32, sc.shape, 0)
                ci = lax.broadcasted_iota(jnp.int32, sc.shape, 1)
                sc = jnp.where(ci <= ri, sc, -jnp.inf)
            m_prev = m_ref[c]
            m_new = jnp.maximum(m_prev, jnp.max(sc, axis=-1, keepdims=True))
            alpha = jnp.exp(m_prev - m_new)
            p = jnp.exp(sc - m_new)
            l_ref[c] = alpha * l_ref[c] + jnp.sum(p, axis=-1, keepdims=True)
            acc_ref[c] = alpha * acc_ref[c] + _dot(p, vb)
            m_ref[c] = m_new

    def full_block(j, carry):
        block(j, False)
        return carry

    lax.fori_loop(0, i, full_block, 0)
    block(i, True)

    lam = lam_ref[...]
    lam_full = (jnp.exp(jnp.sum(lam[0:1] * lam[1:2], axis=-1, keepdims=True))
                - jnp.exp(jnp.sum(lam[2:3] * lam[3:4], axis=-1, keepdims=True)) + lam_init)
    o = acc_ref[0] / l_ref[0] - lam_full * (acc_ref[1] / l_ref[1])
    o_ref[0] = (_rms(o, sub_ref[...]) * (1.0 - lam_init)).astype(o_ref.dtype)


def _diff_attention(q, k, v, lam, subln, lam_init, *, t=512):
    B, S, _ = q.shape
    H = DIFF_HEADS
    kv_spec = pl.BlockSpec((1, S, LANES), lambda b, h, i: (b, 0, h))
    q_spec = pl.BlockSpec((1, t, LANES), lambda b, h, i: (b, i, h))
    return pl.pallas_call(
        functools.partial(_attn_kernel, t=t, lam_init=lam_init),
        grid=(B, H, S // t),
        in_specs=[_resident(lam.shape), q_spec, kv_spec, kv_spec, _resident((1, LANES))],
        out_specs=q_spec,
        out_shape=jax.ShapeDtypeStruct(q.shape, BF16),
        scratch_shapes=[pltpu.VMEM((2, t, 1), F32), pltpu.VMEM((2, t, 1), F32),
                        pltpu.VMEM((2, t, LANES), F32)],
        compiler_params=_params("parallel", "parallel", "parallel"),
        name="diff_attention",
    )(lam, q, k, v, subln[None, :])


def _out_proj_kernel(x_ref, o_ref, w_ref, y_ref):
    y_ref[0] = x_ref[0] + _dot(o_ref[0], w_ref[...])


def _out_proj(x, o, w, *, tm=512):
    B, S, D = x.shape
    tile = lambda n: pl.BlockSpec((1, tm, n), lambda b, s: (b, s, 0))
    return pl.pallas_call(
        _out_proj_kernel,
        grid=(B, S // tm),
        in_specs=[tile(D), tile(o.shape[2]), _resident(w.shape)],
        out_specs=tile(D),
        out_shape=jax.ShapeDtypeStruct(x.shape, F32),
        compiler_params=_params("parallel", "parallel"),
        name="out_proj",
    )(x, o, w.astype(BF16))


def kernel(x, a_norm, a_w_in, a_conv_w, a_log, a_dt_bias, a_out_norm, a_w_out, kv_norm, kv_w, k_norm,
           b_norm, b_w_q, b_q_norm, b_lambda, b_subln, b_w_out, f_norm, f_w_up, f_conv_w, f_conv_b,
           f_w_down):
    n_a = a_norm.shape[0]
    depth = f_norm.shape[0]
    groups = DIFF_HEADS * 2
    k_sh = v_sh = None
    for i in range(depth):
        if i < n_a:
            q, k, v, z, gcol, grow = _gdn_in(x, a_norm[i], a_w_in[i], a_conv_w[i], a_log[i], a_dt_bias[i])
            x = _gdn_chunk(x, q, k, v, z, gcol, grow, a_out_norm[i], a_w_out[i])
        else:
            if i == n_a:
                k_sh, v_sh = _proj(x, kv_norm, kv_w, jnp.tile(k_norm, groups), 2)
            j = i - n_a
            lam_init = 0.8 - 0.6 * math.exp(-0.3 * i)
            (q,) = _proj(x, b_norm[j], b_w_q[j], jnp.tile(b_q_norm[j], groups) * (DIFF_DH ** -0.5), 1)
            o = _diff_attention(q, k_sh, v_sh, b_lambda[j], b_subln[j], lam_init)
            x = _out_proj(x, o, b_w_out[j])
        x = _conv_ffn(x, f_norm[i], f_w_up[i], f_conv_w[i], f_conv_b[i], f_w_down[i])
    return x
```

```python
import functools
import math

import jax
import jax.numpy as jnp
from jax import lax
from jax.experimental import pallas as pl
from jax.experimental.pallas import tpu as pltpu

F32 = jnp.float32
BF16 = jnp.bfloat16

EPS = 1e-6
LANES = 128
HALO = 16
GDN_HEADS = 8
GDN_DK = 128
GDN_CHUNK = 64
GDN_ROWS = 256
DIFF_HEADS = 8
DIFF_DH = 64
ATTN_BLOCK = 512
LOG2E = math.log2(math.e)
FF_COLS = 256
VMEM_LIMIT = 56 * 1024 * 1024


def _params(*sem):
    return pltpu.CompilerParams(dimension_semantics=sem, vmem_limit_bytes=VMEM_LIMIT)


def _resident(shape):
    zeros = (0,) * len(shape)
    return pl.BlockSpec(shape, lambda *_: zeros, pipeline_mode=pl.Buffered(1))


def _rms(x, g):
    ms = jnp.mean(x * x, axis=-1, keepdims=True)
    return x * lax.rsqrt(ms + EPS) * g


def _sigmoid(x):
    return 1.0 / (1.0 + jnp.exp(-x))


def _silu(x):
    return x * _sigmoid(x)


def _dot(a, b):
    return jnp.dot(a.astype(BF16), b.astype(BF16), preferred_element_type=F32)


def _dot_nt(a, b):
    return lax.dot_general(a.astype(BF16), b.astype(BF16), (((1,), (1,)), ((), ())),
                           preferred_element_type=F32)


def _dot_tn(a, b):
    return lax.dot_general(a.astype(BF16), b.astype(BF16), (((0,), (0,)), ((), ())),
                           preferred_element_type=F32)


def _normed_with_halo(x_ref, xh_ref, g_ref):
    g = g_ref[...]
    h = _rms(x_ref[0], g).astype(BF16)
    keep = (pl.program_id(1) > 0).astype(F32)
    hh = (_rms(xh_ref[0], g) * keep).astype(BF16)
    return h, jnp.concatenate([hh, h], axis=0)


def _causal_taps(u, cw, taps):
    y = u[HALO:] * cw[taps - 1:taps]
    for d in range(1, taps):
        y = y + pltpu.roll(u, d, 0)[HALO:] * cw[taps - 1 - d:taps - d]
    return y


def _halo_specs(tm, d):
    tile = pl.BlockSpec((1, tm, d), lambda b, s: (b, s, 0))
    halo = pl.BlockSpec((1, HALO, d), lambda b, s: (b, jnp.maximum(s * (tm // HALO) - 1, 0), 0))
    return tile, halo


def _ffn_kernel(x_ref, xh_ref, g_ref, wg_ref, wu_ref, cwg_ref, cwu_ref, cbg_ref, cbu_ref, wd_ref, o_ref,
                *, n_slabs, taps):
    _, hext = _normed_with_halo(x_ref, xh_ref, g_ref)
    acc = None
    for c in range(n_slabs):
        gate = _causal_taps(_dot(hext, wg_ref[c]), cwg_ref[c], taps) + cbg_ref[c]
        up = _causal_taps(_dot(hext, wu_ref[c]), cwu_ref[c], taps) + cbu_ref[c]
        d = _dot(_silu(gate) * up, wd_ref[c])
        acc = d if acc is None else acc + d
    o_ref[0] = x_ref[0] + acc


def _conv_ffn(x, g, w_up, conv_w, conv_b, w_down, *, tm=512):
    B, S, D = x.shape
    ff = w_down.shape[0]
    taps = conv_w.shape[0]
    n_slabs = ff // FF_COLS
    assert n_slabs * FF_COLS == ff and S % tm == 0

    def slabs(a):
        a = a.reshape(a.shape[0], 2, n_slabs, FF_COLS).transpose(1, 2, 0, 3)
        return a[0], a[1]

    wg, wu = slabs(w_up.astype(BF16))
    cwg, cwu = slabs(conv_w)
    cbg, cbu = slabs(conv_b[None, :])
    wd = w_down.astype(BF16).reshape(n_slabs, FF_COLS, D)
    tile, halo = _halo_specs(tm, D)
    return pl.pallas_call(
        functools.partial(_ffn_kernel, n_slabs=n_slabs, taps=taps),
        grid=(B, S // tm),
        in_specs=[tile, halo, _resident((1, D)),
                  _resident(wg.shape), _resident(wu.shape), _resident(cwg.shape), _resident(cwu.shape),
                  _resident(cbg.shape), _resident(cbu.shape), _resident(wd.shape)],
        out_specs=tile,
        out_shape=jax.ShapeDtypeStruct(x.shape, F32),
        compiler_params=_params("parallel", "parallel"),
        name="conv_ffn",
    )(x, x, g[None, :], wg, wu, cwg, cwu, cbg, cbu, wd)


def _chunk_cumsum(g):
    row = lax.broadcasted_iota(jnp.int32, g.shape, 0) & (GDN_CHUNK - 1)
    step = 1
    while step < GDN_CHUNK:
        g = g + jnp.where(row >= step, pltpu.roll(g, step, 0), 0.0)
        step *= 2
    return g


def _gdn_in_kernel(x_ref, xh_ref, g_ref, wqkv_ref, cw_ref, wz_ref, wab_ref, alog_ref, dtb_ref,
                   q_ref, k_ref, v_ref, z_ref, gcol_ref, grow_ref, *, taps):
    h, hext = _normed_with_halo(x_ref, xh_ref, g_ref)
    qk_cols = GDN_HEADS * GDN_DK
    outs = (q_ref, k_ref, v_ref)
    for c in range(wqkv_ref.shape[1] // 256):
        cols = slice(c * 256, (c + 1) * 256)
        y = _silu(_causal_taps(_dot(hext, wqkv_ref[:, cols]), cw_ref[:, cols], taps))
        for half in range(2):
            col = c * 256 + half * LANES
            yy = y[:, half * LANES:(half + 1) * LANES]
            which, off = divmod(col, qk_cols)
            if which < 2:
                yy = yy * lax.rsqrt(jnp.sum(yy * yy, axis=-1, keepdims=True) + EPS)
            if which == 0:
                yy = yy * (GDN_DK ** -0.5)
            outs[which][0, :, off:off + LANES] = yy.astype(outs[which].dtype)
    for c in range(wz_ref.shape[1] // 256):
        cols = slice(c * 256, (c + 1) * 256)
        z_ref[0, :, cols] = _dot(h, wz_ref[:, cols]).astype(z_ref.dtype)
    ab = _dot(h, wab_ref[...])
    t = ab + dtb_ref[...]
    softplus = jnp.maximum(t, 0.0) + jnp.log1p(jnp.exp(-jnp.abs(t)))
    cum = _chunk_cumsum(-jnp.exp(alog_ref[...]) * softplus)
    lane = lax.broadcasted_iota(jnp.int32, ab.shape, 1)
    gcol_ref[0] = jnp.where(lane < GDN_HEADS, cum, _sigmoid(ab))
    grow_ref[0] = cum.T[:GDN_HEADS, :]


def _gdn_in(x, g, w_in, conv_w, a_log, dt_bias, *, tm=512, act_dtype=BF16):
    B, S, D = x.shape
    H = GDN_HEADS
    qk = H * GDN_DK
    nqkv = conv_w.shape[1]
    nz = nqkv - 2 * qk
    wqkv = w_in[:, :nqkv].astype(BF16)
    wz = w_in[:, nqkv:nqkv + nz].astype(BF16)
    wab = jnp.pad(w_in[:, nqkv + nz:], ((0, 0), (0, LANES - 2 * H))).astype(BF16)
    alog = jnp.pad(a_log, (0, LANES - H))[None, :]
    dtb = jnp.pad(dt_bias, (0, LANES - H))[None, :]
    tile, halo = _halo_specs(tm, D)
    act = lambda n: jax.ShapeDtypeStruct((B, S, n), act_dtype)
    act_spec = lambda n: pl.BlockSpec((1, tm, n), lambda b, s: (b, s, 0))
    return pl.pallas_call(
        functools.partial(_gdn_in_kernel, taps=conv_w.shape[0]),
        grid=(B, S // tm),
        in_specs=[tile, halo, _resident((1, D)), _resident(wqkv.shape), _resident(conv_w.shape),
                  _resident(wz.shape), _resident(wab.shape), _resident(alog.shape), _resident(dtb.shape)],
        out_specs=[act_spec(qk), act_spec(qk), act_spec(nz), act_spec(nz), act_spec(LANES),
                   pl.BlockSpec((1, H, tm), lambda b, s: (b, 0, s))],
        out_shape=[act(qk), act(qk), act(nz), act(nz),
                   jax.ShapeDtypeStruct((B, S, LANES), F32), jax.ShapeDtypeStruct((B, H, S), F32)],
        compiler_params=_params("parallel", "parallel"),
        name="gdn_in",
    )(x, x, g[None, :], wqkv, conv_w, wz, wab, alog, dtb)


def _unit_lower_inverse(a, ri, ci):
    in16 = (ri >> 4) == (ci >> 4)
    in32 = (ri >> 5) == (ci >> 5)
    d = jnp.where(in16, a, 0.0)
    d2 = _dot(d, d)
    d4 = _dot(d2, d2)
    d8 = _dot(d4, d4)
    x = jnp.where(ri == ci, 1.0, 0.0) - d
    x = x + _dot(x, d2)
    x = x + _dot(x, d4)
    x = x + _dot(x, d8)
    a32 = jnp.where(in32, a, 0.0)
    x = x - _dot(x, _dot(a32 - d, x))
    x = x - _dot(x, _dot(a - a32, x))
    return x


def _gdn_chunk_kernel(q_ref, k_ref, v_ref, z_ref, gcol_ref, grow_ref, x_ref, onorm_ref, wout_ref,
                      o_ref, state_ref, acc_ref, *, heads_per_step):
    s = pl.program_id(1)
    hg = pl.program_id(2)
    R, C = GDN_ROWS, GDN_CHUNK
    ri = lax.broadcasted_iota(jnp.int32, (R, R), 0)
    ci = lax.broadcasted_iota(jnp.int32, (R, R), 1)
    same = (ri >> 6) == (ci >> 6)
    causal = same & (ri >= ci)
    strict = same & (ri > ci)
    gcol = gcol_ref[0]
    lane = lax.broadcasted_iota(jnp.int32, gcol.shape, 1)

    total = None
    for j in range(heads_per_step):
        h = hg * heads_per_step + j
        lanes = slice(j * LANES, (j + 1) * LANES)
        q = q_ref[0, :, lanes].astype(F32)
        k = k_ref[0, :, lanes].astype(F32)
        v = v_ref[0, :, lanes].astype(F32)
        gc = jnp.sum(jnp.where(lane == h, gcol, 0.0), axis=1, keepdims=True)
        beta = jnp.sum(jnp.where(lane == h + GDN_HEADS, gcol, 0.0), axis=1, keepdims=True)
        gc_row = grow_ref[0, pl.ds(h, 1), :]
        decay = jnp.exp(jnp.where(causal, gc - gc_row, -jnp.inf))
        kb = k.astype(BF16)
        a = jnp.where(strict, _dot_nt(kb, kb) * decay * beta, 0.0)
        tinv = _unit_lower_inverse(a, ri, ci)
        attn = _dot_nt(q, kb) * decay
        egc = jnp.exp(gc)
        g_last = jnp.concatenate(
            [jnp.broadcast_to(gc[c * C + C - 1:(c + 1) * C, :], (C, 1)) for c in range(R // C)], axis=0)
        sol = _dot(tinv, jnp.concatenate([v * beta, k * (beta * egc)], axis=1))
        u, w = sol[:, :LANES], sol[:, LANES:]
        q_dec = q * egc
        k_dec = k * jnp.exp(g_last - gc)

        @pl.when(s == 0)
        def _():
            state_ref[h] = jnp.zeros(state_ref.shape[1:], F32)

        state = state_ref[h]
        corrected, from_state = [], []
        for c in range(R // C):
            rows = slice(c * C, (c + 1) * C)
            r = _dot(jnp.concatenate([w[rows], q_dec[rows]], axis=0), state)
            uc = u[rows] - r[:C]
            corrected.append(uc)
            from_state.append(r[C:])
            state = state * jnp.exp(g_last[c * C:c * C + 1, :]) + _dot_tn(k_dec[rows], uc)
        state_ref[h] = state
        o = jnp.concatenate(from_state, axis=0) + _dot(attn, jnp.concatenate(corrected, axis=0))
        o = _rms(o, onorm_ref[...]) * _silu(z_ref[0, :, lanes].astype(F32))
        contrib = _dot(o, wout_ref[h])
        total = contrib if total is None else total + contrib

    @pl.when(hg == 0)
    def _():
        acc_ref[...] = total

    @pl.when(hg > 0)
    def _():
        acc_ref[...] += total

    @pl.when(hg == pl.num_programs(2) - 1)
    def _():
        o_ref[0] = x_ref[0] + acc_ref[...]


def _gdn_chunk(x, q, k, v, z, gcol, grow, out_norm, w_out, *, heads_per_step=1):
    B, S, D = x.shape
    H, R = GDN_HEADS, GDN_ROWS
    dv = w_out.shape[0] // H
    hw = heads_per_step * LANES
    wout = w_out.astype(BF16).reshape(H, dv, D)
    head = pl.BlockSpec((1, R, hw), lambda b, s, h: (b, s, h))
    full = pl.BlockSpec((1, R, D), lambda b, s, h: (b, s, 0))
    return pl.pallas_call(
        functools.partial(_gdn_chunk_kernel, heads_per_step=heads_per_step),
        grid=(B, S // R, H // heads_per_step),
        in_specs=[head, head, head, head,
                  pl.BlockSpec((1, R, LANES), lambda b, s, h: (b, s, 0)),
                  pl.BlockSpec((1, H, R), lambda b, s, h: (b, 0, s)),
                  full, _resident((1, dv)), _resident(wout.shape)],
        out_specs=full,
        out_shape=jax.ShapeDtypeStruct(x.shape, F32),
        scratch_shapes=[pltpu.VMEM((H, GDN_DK, dv), F32), pltpu.VMEM((R, D), F32)],
        compiler_params=_params("arbitrary", "arbitrary", "arbitrary"),
        name="gdn_chunk",
    )(q, k, v, z, gcol, grow, x, out_norm[None, :], wout)


def _proj_kernel(x_ref, g_ref, w_ref, gain_ref, o_ref, ot_ref=None, *, n_normed):
    h = _rms(x_ref[0], g_ref[...]).astype(BF16)
    width = o_ref.shape[2]
    for c in range(w_ref.shape[1] // 256):
        y = _dot(h, w_ref[:, c * 256:(c + 1) * 256])
        for half in range(2):
            col = c * 256 + half * LANES
            yy = y[:, half * LANES:(half + 1) * LANES]
            if col < n_normed:
                lane = lax.broadcasted_iota(jnp.int32, yy.shape, 1)
                low = lane < DIFF_DH
                sq = yy * yy
                ms = jnp.where(low,
                               jnp.sum(jnp.where(low, sq, 0.0), axis=-1, keepdims=True),
                               jnp.sum(jnp.where(low, 0.0, sq), axis=-1, keepdims=True)) * (1.0 / DIFF_DH)
                yy = yy * lax.rsqrt(ms + EPS) * gain_ref[:, col:col + LANES]
            if col < width:
                o_ref[0, :, col:col + LANES] = yy.astype(o_ref.dtype)
            else:
                ot_ref[0, (col - width) // LANES, 0] = yy.T.astype(ot_ref.dtype)


def _proj(x, g, w, gain, *, tm):
    B, S, D = x.shape
    width = gain.shape[0]
    heads_t = (w.shape[1] - width) // LANES
    tile = pl.BlockSpec((1, tm, D), lambda b, s: (b, s, 0))
    out_specs = [pl.BlockSpec((1, tm, width), lambda b, s: (b, s, 0))]
    out_shape = [jax.ShapeDtypeStruct((B, S, width), BF16)]
    if heads_t:
        out_specs.append(pl.BlockSpec((1, heads_t, 1, LANES, tm), lambda b, s: (b, 0, s, 0, 0)))
        out_shape.append(jax.ShapeDtypeStruct((B, heads_t, S // tm, LANES, tm), BF16))
    return pl.pallas_call(
        functools.partial(_proj_kernel, n_normed=width),
        grid=(B, S // tm),
        in_specs=[tile, _resident((1, D)), _resident(w.shape), _resident((1, width))],
        out_specs=out_specs,
        out_shape=out_shape,
        compiler_params=_params("parallel", "parallel"),
        name="norm_proj",
    )(x, g[None, :], w.astype(BF16), gain[None, :])


def _attn_kernel(lam_ref, q_ref, k_ref, vt_ref, sub_ref, o_ref, m_ref, l_ref, acc_ref, *, t, lam_init):
    i = pl.program_id(2)
    q = q_ref[0]
    lane = lax.broadcasted_iota(jnp.int32, q.shape, 1)
    qs = (jnp.where(lane < DIFF_DH, q, jnp.zeros_like(q)), jnp.where(lane < DIFF_DH, jnp.zeros_like(q), q))
    m_ref[...] = jnp.full(m_ref.shape, -jnp.inf, F32)
    l_ref[...] = jnp.zeros(l_ref.shape, F32)
    acc_ref[...] = jnp.zeros(acc_ref.shape, F32)

    def block(j, masked):
        kb = k_ref[0, pl.ds(pl.multiple_of(j * t, t), t), :]
        vt = vt_ref[0, 0, j]
        for c in range(2):
            sc = _dot_nt(kb, qs[c])
            if masked:
                key = lax.broadcasted_iota(jnp.int32, sc.shape, 0)
                qry = lax.broadcasted_iota(jnp.int32, sc.shape, 1)
                sc = jnp.where(key <= qry, sc, -jnp.inf)
            m_prev = m_ref[c]
            m_new = jnp.maximum(m_prev, jnp.max(sc, axis=0, keepdims=True))
            alpha = jnp.exp2(m_prev - m_new)
            p = jnp.exp2(sc - m_new)
            l_ref[c] = alpha * l_ref[c] + jnp.sum(p, axis=0, keepdims=True)
            acc_ref[c] = alpha * acc_ref[c] + _dot(vt, p)
            m_ref[c] = m_new

    def full_block(j, carry):
        block(j, False)
        return carry

    lax.fori_loop(0, i, full_block, 0)
    block(i, True)

    lam = lam_ref[...]
    lam_full = (jnp.exp(jnp.sum(lam[0:1] * lam[1:2], axis=-1, keepdims=True))
                - jnp.exp(jnp.sum(lam[2:3] * lam[3:4], axis=-1, keepdims=True)) + lam_init)
    o = acc_ref[0] / l_ref[0] - lam_full * (acc_ref[1] / l_ref[1])
    ms = jnp.mean(o * o, axis=0, keepdims=True)
    o = o * lax.rsqrt(ms + EPS) * (sub_ref[...] * (1.0 - lam_init))
    o_ref[0] = o.T.astype(o_ref.dtype)


def _diff_attention(q, k, vt, lam, subln, lam_init):
    B, S, _ = q.shape
    H = DIFF_HEADS
    nk, dv, t = vt.shape[2:]
    k_spec = pl.BlockSpec((1, S, LANES), lambda b, h, i: (b, 0, h))
    vt_spec = pl.BlockSpec((1, 1, nk, dv, t), lambda b, h, i: (b, h, 0, 0, 0))
    q_spec = pl.BlockSpec((1, t, LANES), lambda b, h, i: (b, i, h))
    return pl.pallas_call(
        functools.partial(_attn_kernel, t=t, lam_init=lam_init),
        grid=(B, H, S // t),
        in_specs=[_resident(lam.shape), q_spec, k_spec, vt_spec, _resident((dv, 1))],
        out_specs=q_spec,
        out_shape=jax.ShapeDtypeStruct(q.shape, BF16),
        scratch_shapes=[pltpu.VMEM((2, 1, t), F32), pltpu.VMEM((2, 1, t), F32), pltpu.VMEM((2, dv, t), F32)],
        compiler_params=_params("parallel", "parallel", "parallel"),
        name="diff_attention",
    )(lam, q, k, vt, subln[:, None])


def _out_proj_kernel(x_ref, o_ref, w_ref, y_ref):
    y_ref[0] = x_ref[0] + _dot(o_ref[0], w_ref[...])


def _out_proj(x, o, w, *, tm=512):
    B, S, D = x.shape
    tile = lambda n: pl.BlockSpec((1, tm, n), lambda b, s: (b, s, 0))
    return pl.pallas_call(
        _out_proj_kernel,
        grid=(B, S // tm),
        in_specs=[tile(D), tile(o.shape[2]), _resident(w.shape)],
        out_specs=tile(D),
        out_shape=jax.ShapeDtypeStruct(x.shape, F32),
        compiler_params=_params("parallel", "parallel"),
        name="out_proj",
    )(x, o, w.astype(BF16))


def kernel(x, a_norm, a_w_in, a_conv_w, a_log, a_dt_bias, a_out_norm, a_w_out, kv_norm, kv_w, k_norm,
           b_norm, b_w_q, b_q_norm, b_lambda, b_subln, b_w_out, f_norm, f_w_up, f_conv_w, f_conv_b,
           f_w_down):
    n_a = a_norm.shape[0]
    depth = f_norm.shape[0]
    groups = DIFF_HEADS * 2
    k_sh = v_sh = None
    for i in range(depth):
        if i < n_a:
            q, k, v, z, gcol, grow = _gdn_in(x, a_norm[i], a_w_in[i], a_conv_w[i], a_log[i], a_dt_bias[i])
            x = _gdn_chunk(x, q, k, v, z, gcol, grow, a_out_norm[i], a_w_out[i])
        else:
            if i == n_a:
                k_sh, vt_sh = _proj(x, kv_norm, kv_w, jnp.tile(k_norm, groups), tm=ATTN_BLOCK)
            j = i - n_a
            lam_init = 0.8 - 0.6 * math.exp(-0.3 * i)
            q_gain = jnp.tile(b_q_norm[j], groups) * (DIFF_DH ** -0.5 * LOG2E)
            (q,) = _proj(x, b_norm[j], b_w_q[j], q_gain, tm=ATTN_BLOCK)
            o = _diff_attention(q, k_sh, vt_sh, b_lambda[j], b_subln[j], lam_init)
            x = _out_proj(x, o, b_w_out[j])
        x = _conv_ffn(x, f_norm[i], f_w_up[i], f_conv_w[i], f_conv_b[i], f_w_down[i])
    return x
```

```python
import functools
import math

import jax
import jax.numpy as jnp
from jax import lax
from jax.experimental import pallas as pl
from jax.experimental.pallas import tpu as pltpu

F32 = jnp.float32
BF16 = jnp.bfloat16

EPS = 1e-6
LANES = 128
HALO = 16
GDN_HEADS = 8
GDN_DK = 128
GDN_CHUNK = 64
GDN_ROWS = 256
DIFF_HEADS = 8
DIFF_DH = 64
ATTN_BLOCK = 512
LOG2E = math.log2(math.e)
ONES_ROWS = 16
FF_COLS = 256
VMEM_LIMIT = 56 * 1024 * 1024


def _params(*sem):
    return pltpu.CompilerParams(dimension_semantics=sem, vmem_limit_bytes=VMEM_LIMIT)


def _resident(shape):
    zeros = (0,) * len(shape)
    return pl.BlockSpec(shape, lambda *_: zeros, pipeline_mode=pl.Buffered(1))


def _rms(x, g):
    ms = jnp.mean(x * x, axis=-1, keepdims=True)
    return x * lax.rsqrt(ms + EPS) * g


def _sigmoid(x):
    return 1.0 / (1.0 + jnp.exp(-x))


def _silu(x):
    return x * _sigmoid(x)


def _dot(a, b):
    return jnp.dot(a.astype(BF16), b.astype(BF16), preferred_element_type=F32)


def _dot_nt(a, b):
    return lax.dot_general(a.astype(BF16), b.astype(BF16), (((1,), (1,)), ((), ())),
                           preferred_element_type=F32)


def _dot_tn(a, b):
    return lax.dot_general(a.astype(BF16), b.astype(BF16), (((0,), (0,)), ((), ())),
                           preferred_element_type=F32)


def _normed_with_halo(x_ref, xh_ref, g_ref):
    g = g_ref[...]
    h = _rms(x_ref[0], g).astype(BF16)
    keep = (pl.program_id(1) > 0).astype(F32)
    hh = (_rms(xh_ref[0], g) * keep).astype(BF16)
    return h, jnp.concatenate([hh, h], axis=0)


def _causal_taps(u, cw, taps):
    y = u[HALO:] * cw[taps - 1:taps]
    for d in range(1, taps):
        y = y + pltpu.roll(u, d, 0)[HALO:] * cw[taps - 1 - d:taps - d]
    return y


def _halo_specs(tm, d):
    tile = pl.BlockSpec((1, tm, d), lambda b, s: (b, s, 0))
    halo = pl.BlockSpec((1, HALO, d), lambda b, s: (b, jnp.maximum(s * (tm // HALO) - 1, 0), 0))
    return tile, halo


def _ffn_kernel(x_ref, xh_ref, g_ref, wg_ref, wu_ref, cwg_ref, cwu_ref, cbg_ref, cbu_ref, wd_ref, o_ref,
                *, n_slabs, taps):
    _, hext = _normed_with_halo(x_ref, xh_ref, g_ref)

    def up_proj(c):
        return _dot(hext, wg_ref[c]), _dot(hext, wu_ref[c])

    acc = None
    ahead = up_proj(0)
    for c in range(n_slabs):
        ug, uu = ahead
        if c + 1 < n_slabs:
            ahead = up_proj(c + 1)
        gate = _causal_taps(ug, cwg_ref[c], taps) + cbg_ref[c]
        up = _causal_taps(uu, cwu_ref[c], taps) + cbu_ref[c]
        d = _dot(_silu(gate) * up, wd_ref[c])
        acc = d if acc is None else acc + d
    o_ref[0] = x_ref[0] + acc


def _conv_ffn(x, g, w_up, conv_w, conv_b, w_down, *, tm=512):
    B, S, D = x.shape
    ff = w_down.shape[0]
    taps = conv_w.shape[0]
    n_slabs = ff // FF_COLS
    assert n_slabs * FF_COLS == ff and S % tm == 0

    def slabs(a):
        a = a.reshape(a.shape[0], 2, n_slabs, FF_COLS).transpose(1, 2, 0, 3)
        return a[0], a[1]

    wg, wu = slabs(w_up.astype(BF16))
    cwg, cwu = slabs(conv_w)
    cbg, cbu = slabs(conv_b[None, :])
    wd = w_down.astype(BF16).reshape(n_slabs, FF_COLS, D)
    tile, halo = _halo_specs(tm, D)
    return pl.pallas_call(
        functools.partial(_ffn_kernel, n_slabs=n_slabs, taps=taps),
        grid=(B, S // tm),
        in_specs=[tile, halo, _resident((1, D)),
                  _resident(wg.shape), _resident(wu.shape), _resident(cwg.shape), _resident(cwu.shape),
                  _resident(cbg.shape), _resident(cbu.shape), _resident(wd.shape)],
        out_specs=tile,
        out_shape=jax.ShapeDtypeStruct(x.shape, F32),
        compiler_params=_params("parallel", "parallel"),
        name="conv_ffn",
    )(x, x, g[None, :], wg, wu, cwg, cwu, cbg, cbu, wd)


def _chunk_cumsum(g):
    row = lax.broadcasted_iota(jnp.int32, g.shape, 0) & (GDN_CHUNK - 1)
    step = 1
    while step < GDN_CHUNK:
        g = g + jnp.where(row >= step, pltpu.roll(g, step, 0), 0.0)
        step *= 2
    return g


def _gdn_in_kernel(x_ref, xh_ref, g_ref, wqkv_ref, cw_ref, wz_ref, wab_ref, alog_ref, dtb_ref,
                   q_ref, k_ref, v_ref, z_ref, gcol_ref, grow_ref, *, taps):
    h, hext = _normed_with_halo(x_ref, xh_ref, g_ref)
    qk_cols = GDN_HEADS * GDN_DK
    outs = (q_ref, k_ref, v_ref)
    for c in range(wqkv_ref.shape[1] // 256):
        cols = slice(c * 256, (c + 1) * 256)
        y = _silu(_causal_taps(_dot(hext, wqkv_ref[:, cols]), cw_ref[:, cols], taps))
        for half in range(2):
            col = c * 256 + half * LANES
            yy = y[:, half * LANES:(half + 1) * LANES]
            which, off = divmod(col, qk_cols)
            if which < 2:
                yy = yy * lax.rsqrt(jnp.sum(yy * yy, axis=-1, keepdims=True) + EPS)
            if which == 0:
                yy = yy * (GDN_DK ** -0.5)
            outs[which][0, :, off:off + LANES] = yy.astype(outs[which].dtype)
    for c in range(wz_ref.shape[1] // 256):
        cols = slice(c * 256, (c + 1) * 256)
        z_ref[0, :, cols] = _dot(h, wz_ref[:, cols]).astype(z_ref.dtype)
    ab = _dot(h, wab_ref[...])
    t = ab + dtb_ref[...]
    softplus = jnp.maximum(t, 0.0) + jnp.log1p(jnp.exp(-jnp.abs(t)))
    cum = _chunk_cumsum(-jnp.exp(alog_ref[...]) * softplus)
    lane = lax.broadcasted_iota(jnp.int32, ab.shape, 1)
    gcol_ref[0] = jnp.where(lane < GDN_HEADS, cum, _sigmoid(ab))
    grow_ref[0] = cum.T[:GDN_HEADS, :]


def _gdn_in(x, g, w_in, conv_w, a_log, dt_bias, *, tm=512, act_dtype=BF16):
    B, S, D = x.shape
    H = GDN_HEADS
    qk = H * GDN_DK
    nqkv = conv_w.shape[1]
    nz = nqkv - 2 * qk
    wqkv = w_in[:, :nqkv].astype(BF16)
    wz = w_in[:, nqkv:nqkv + nz].astype(BF16)
    wab = jnp.pad(w_in[:, nqkv + nz:], ((0, 0), (0, LANES - 2 * H))).astype(BF16)
    alog = jnp.pad(a_log, (0, LANES - H))[None, :]
    dtb = jnp.pad(dt_bias, (0, LANES - H))[None, :]
    tile, halo = _halo_specs(tm, D)
    act = lambda n: jax.ShapeDtypeStruct((B, S, n), act_dtype)
    act_spec = lambda n: pl.BlockSpec((1, tm, n), lambda b, s: (b, s, 0))
    return pl.pallas_call(
        functools.partial(_gdn_in_kernel, taps=conv_w.shape[0]),
        grid=(B, S // tm),
        in_specs=[tile, halo, _resident((1, D)), _resident(wqkv.shape), _resident(conv_w.shape),
                  _resident(wz.shape), _resident(wab.shape), _resident(alog.shape), _resident(dtb.shape)],
        out_specs=[act_spec(qk), act_spec(qk), act_spec(nz), act_spec(nz), act_spec(LANES),
                   pl.BlockSpec((1, H, tm), lambda b, s: (b, 0, s))],
        out_shape=[act(qk), act(qk), act(nz), act(nz),
                   jax.ShapeDtypeStruct((B, S, LANES), F32), jax.ShapeDtypeStruct((B, H, S), F32)],
        compiler_params=_params("parallel", "parallel"),
        name="gdn_in",
    )(x, x, g[None, :], wqkv, conv_w, wz, wab, alog, dtb)


def _each(f, *lists):
    return [f(*args) for args in zip(*lists)]


def _unit_lower_inverses(a_s, ri, ci):
    in16 = (ri >> 4) == (ci >> 4)
    in32 = (ri >> 5) == (ci >> 5)
    eye = jnp.where(ri == ci, 1.0, 0.0)
    d = _each(lambda a: jnp.where(in16, a, 0.0), a_s)
    d2 = _each(_dot, d, d)
    x = _each(lambda d_: eye - d_, d)
    d4 = _each(_dot, d2, d2)
    x = _each(lambda x_, p: x_ + _dot(x_, p), x, d2)
    d8 = _each(_dot, d4, d4)
    x = _each(lambda x_, p: x_ + _dot(x_, p), x, d4)
    x = _each(lambda x_, p: x_ + _dot(x_, p), x, d8)
    a32 = _each(lambda a: jnp.where(in32, a, 0.0), a_s)
    t = _each(lambda lo, d_, x_: _dot(lo - d_, x_), a32, d, x)
    x = _each(lambda x_, t_: x_ - _dot(x_, t_), x, t)
    t = _each(lambda a, lo, x_: _dot(a - lo, x_), a_s, a32, x)
    return _each(lambda x_, t_: x_ - _dot(x_, t_), x, t)


def _gdn_chunk_kernel(q_ref, k_ref, v_ref, z_ref, gcol_ref, grow_ref, x_ref, onorm_ref, wout_ref,
                      o_ref, state_ref):
    R, C = GDN_ROWS, GDN_CHUNK
    ri = lax.broadcasted_iota(jnp.int32, (R, R), 0)
    ci = lax.broadcasted_iota(jnp.int32, (R, R), 1)
    same = (ri >> 6) == (ci >> 6)
    causal = same & (ri >= ci)
    strict = same & (ri > ci)
    gcol = gcol_ref[0]

    @pl.when(pl.program_id(1) == 0)
    def _():
        state_ref[...] = jnp.zeros(state_ref.shape, F32)

    heads = list(range(GDN_HEADS))
    head_lanes = [slice(h * LANES, (h + 1) * LANES) for h in heads]
    q = [q_ref[0, :, ln].astype(F32) for ln in head_lanes]
    k = [k_ref[0, :, ln].astype(F32) for ln in head_lanes]
    v = [v_ref[0, :, ln].astype(F32) for ln in head_lanes]
    gc = [gcol[:, h:h + 1] for h in heads]
    beta = [gcol[:, GDN_HEADS + h:GDN_HEADS + h + 1] for h in heads]
    decay = [jnp.exp(jnp.where(causal, g - grow_ref[0, h:h + 1, :], -jnp.inf)) for g, h in zip(gc, heads)]
    kb = [k_.astype(BF16) for k_ in k]
    kk = _each(_dot_nt, kb, kb)
    a = _each(lambda kk_, d, b: jnp.where(strict, kk_ * d * b, 0.0), kk, decay, beta)
    tinv = _unit_lower_inverses(a, ri, ci)
    attn = _each(lambda q_, kb_, d: _dot_nt(q_, kb_) * d, q, kb, decay)
    egc = [jnp.exp(g) for g in gc]
    g_last = [jnp.concatenate([jnp.broadcast_to(g[c * C + C - 1:(c + 1) * C, :], (C, 1)) for c in range(R // C)],
                              axis=0) for g in gc]
    sol = _each(lambda t, v_, k_, b, e: _dot(t, jnp.concatenate([v_ * b, k_ * (b * e)], axis=1)),
                tinv, v, k, beta, egc)
    u = [s_[:, :LANES] for s_ in sol]
    w = [s_[:, LANES:] for s_ in sol]
    q_dec = _each(lambda q_, e: q_ * e, q, egc)
    k_dec = _each(lambda k_, gl, g: k_ * jnp.exp(gl - g), k, g_last, gc)

    state = [state_ref[h] for h in heads]
    corrected = [[] for _ in heads]
    from_state = [[] for _ in heads]
    for c in range(R // C):
        rows = slice(c * C, (c + 1) * C)
        r = _each(lambda w_, qd, st: _dot(jnp.concatenate([w_[rows], qd[rows]], axis=0), st), w, q_dec, state)
        uc = _each(lambda u_, r_: u_[rows] - r_[:C], u, r)
        for j in heads:
            corrected[j].append(uc[j])
            from_state[j].append(r[j][C:])
        state = _each(lambda st, gl, kd, uc_: st * jnp.exp(gl[c * C:c * C + 1, :]) + _dot_tn(kd[rows], uc_),
                      state, g_last, k_dec, uc)
    for h in heads:
        state_ref[h] = state[h]
    o = _each(lambda fs, at, co: jnp.concatenate(fs, axis=0) + _dot(at, jnp.concatenate(co, axis=0)),
              from_state, attn, corrected)
    o = _each(lambda o_, ln: _rms(o_, onorm_ref[...]) * _silu(z_ref[0, :, ln].astype(F32)), o, head_lanes)
    o_ref[0] = x_ref[0] + _dot(jnp.concatenate([o_.astype(BF16) for o_ in o], axis=1), wout_ref[...])


def _gdn_chunk(x, q, k, v, z, gcol, grow, out_norm, w_out):
    B, S, D = x.shape
    H, R = GDN_HEADS, GDN_ROWS
    dv = w_out.shape[0] // H
    row_block = lambda n: pl.BlockSpec((1, R, n), lambda b, s: (b, s, 0))
    return pl.pallas_call(
        _gdn_chunk_kernel,
        grid=(B, S // R),
        in_specs=[row_block(q.shape[2]), row_block(k.shape[2]), row_block(v.shape[2]), row_block(z.shape[2]),
                  row_block(LANES), pl.BlockSpec((1, H, R), lambda b, s: (b, 0, s)),
                  row_block(D), _resident((1, dv)), _resident(w_out.shape)],
        out_specs=row_block(D),
        out_shape=jax.ShapeDtypeStruct(x.shape, F32),
        scratch_shapes=[pltpu.VMEM((H, GDN_DK, dv), F32)],
        compiler_params=_params("parallel", "arbitrary"),
        name="gdn_chunk",
    )(q, k, v, z, gcol, grow, x, out_norm[None, :], w_out.astype(BF16))


def _proj_kernel(x_ref, g_ref, w_ref, gain_ref, o_ref, ot_ref=None, *, n_normed):
    h = _rms(x_ref[0], g_ref[...]).astype(BF16)
    width = o_ref.shape[2]
    for c in range(w_ref.shape[1] // 256):
        y = _dot(h, w_ref[:, c * 256:(c + 1) * 256])
        for half in range(2):
            col = c * 256 + half * LANES
            yy = y[:, half * LANES:(half + 1) * LANES]
            if col < n_normed:
                lane = lax.broadcasted_iota(jnp.int32, yy.shape, 1)
                low = lane < DIFF_DH
                sq = yy * yy
                ms = jnp.where(low,
                               jnp.sum(jnp.where(low, sq, 0.0), axis=-1, keepdims=True),
                               jnp.sum(jnp.where(low, 0.0, sq), axis=-1, keepdims=True)) * (1.0 / DIFF_DH)
                yy = yy * lax.rsqrt(ms + EPS) * gain_ref[:, col:col + LANES]
            if col < width:
                o_ref[0, :, col:col + LANES] = yy.astype(o_ref.dtype)
            else:
                head = (col - width) // LANES
                ot_ref[0, head, 0, :LANES, :] = yy.T.astype(ot_ref.dtype)
                ot_ref[0, head, 0, LANES:, :] = jnp.ones((ONES_ROWS, yy.shape[0]), ot_ref.dtype)


def _proj(x, g, w, gain, *, tm):
    B, S, D = x.shape
    width = gain.shape[0]
    heads_t = (w.shape[1] - width) // LANES
    tile = pl.BlockSpec((1, tm, D), lambda b, s: (b, s, 0))
    out_specs = [pl.BlockSpec((1, tm, width), lambda b, s: (b, s, 0))]
    out_shape = [jax.ShapeDtypeStruct((B, S, width), BF16)]
    if heads_t:
        rows = LANES + ONES_ROWS
        out_specs.append(pl.BlockSpec((1, heads_t, 1, rows, tm), lambda b, s: (b, 0, s, 0, 0)))
        out_shape.append(jax.ShapeDtypeStruct((B, heads_t, S // tm, rows, tm), BF16))
    return pl.pallas_call(
        functools.partial(_proj_kernel, n_normed=width),
        grid=(B, S // tm),
        in_specs=[tile, _resident((1, D)), _resident(w.shape), _resident((1, width))],
        out_specs=out_specs,
        out_shape=out_shape,
        compiler_params=_params("parallel", "parallel"),
        name="norm_proj",
    )(x, g[None, :], w.astype(BF16), gain[None, :])


def _attn_kernel(lam_ref, q_ref, k_ref, vt_ref, sub_ref, o_ref, m_ref, acc_ref, sa_ref, sb_ref, *, t, lam_init):
    i = pl.program_id(2)
    dv = o_ref.shape[2]
    q = q_ref[0]
    lane = lax.broadcasted_iota(jnp.int32, q.shape, 1)
    qs = (jnp.where(lane < DIFF_DH, q, jnp.zeros_like(q)), jnp.where(lane < DIFF_DH, jnp.zeros_like(q), q))
    m_ref[...] = jnp.full(m_ref.shape, -jnp.inf, F32)
    acc_ref[...] = jnp.zeros(acc_ref.shape, F32)

    def scores(j, dst_ref):
        kb = k_ref[0, pl.ds(pl.multiple_of(j * t, t), t), :]
        for c in range(2):
            dst_ref[c] = _dot_nt(kb, qs[c])

    def consume(src_ref, j, masked):
        vt = vt_ref[0, 0, j]
        for c in range(2):
            sc = src_ref[c]
            if masked:
                key = lax.broadcasted_iota(jnp.int32, sc.shape, 0)
                qry = lax.broadcasted_iota(jnp.int32, sc.shape, 1)
                sc = jnp.where(key <= qry, sc, -jnp.inf)
            m_prev = m_ref[c]
            m_new = jnp.maximum(m_prev, jnp.max(sc, axis=0, keepdims=True))
            p = jnp.exp2(sc - m_new)
            acc_ref[c] = jnp.exp2(m_prev - m_new) * acc_ref[c] + _dot(vt, p)
            m_ref[c] = m_new

    scores(0, sa_ref)

    def two_blocks(pair, carry):
        j = 2 * pair
        scores(j + 1, sb_ref)
        consume(sa_ref, j, False)
        scores(j + 2, sa_ref)
        consume(sb_ref, j + 1, False)
        return carry

    lax.fori_loop(0, i // 2, two_blocks, 0)

    @pl.when(i % 2 == 1)
    def _():
        scores(i, sb_ref)
        consume(sa_ref, i - 1, False)
        consume(sb_ref, i, True)

    @pl.when(i % 2 == 0)
    def _():
        consume(sa_ref, i, True)

    lam = lam_ref[...]
    lam_full = (jnp.exp(jnp.sum(lam[0:1] * lam[1:2], axis=-1, keepdims=True))
                - jnp.exp(jnp.sum(lam[2:3] * lam[3:4], axis=-1, keepdims=True)) + lam_init)
    o = (acc_ref[0, :dv] / acc_ref[0, dv:dv + 1] - lam_full * (acc_ref[1, :dv] / acc_ref[1, dv:dv + 1]))
    ms = jnp.mean(o * o, axis=0, keepdims=True)
    o = o * lax.rsqrt(ms + EPS) * (sub_ref[...] * (1.0 - lam_init))
    o_ref[0] = o.T.astype(o_ref.dtype)


def _diff_attention(q, k, vt, lam, subln, lam_init):
    B, S, _ = q.shape
    H = DIFF_HEADS
    nk, rows, t = vt.shape[2:]
    dv = subln.shape[0]
    k_spec = pl.BlockSpec((1, S, LANES), lambda b, h, i: (b, 0, h))
    vt_spec = pl.BlockSpec((1, 1, nk, rows, t), lambda b, h, i: (b, h, 0, 0, 0))
    q_spec = pl.BlockSpec((1, t, LANES), lambda b, h, i: (b, i, h))
    return pl.pallas_call(
        functools.partial(_attn_kernel, t=t, lam_init=lam_init),
        grid=(B, H, S // t),
        in_specs=[_resident(lam.shape), q_spec, k_spec, vt_spec, _resident((dv, 1))],
        out_specs=q_spec,
        out_shape=jax.ShapeDtypeStruct(q.shape, BF16),
        scratch_shapes=[pltpu.VMEM((2, 1, t), F32), pltpu.VMEM((2, rows, t), F32),
                        pltpu.VMEM((2, t, t), F32), pltpu.VMEM((2, t, t), F32)],
        compiler_params=_params("parallel", "parallel", "parallel"),
        name="diff_attention",
    )(lam, q, k, vt, subln[:, None])


def _out_proj_kernel(x_ref, o_ref, w_ref, y_ref):
    y_ref[0] = x_ref[0] + _dot(o_ref[0], w_ref[...])


def _out_proj(x, o, w, *, tm=512):
    B, S, D = x.shape
    tile = lambda n: pl.BlockSpec((1, tm, n), lambda b, s: (b, s, 0))
    return pl.pallas_call(
        _out_proj_kernel,
        grid=(B, S // tm),
        in_specs=[tile(D), tile(o.shape[2]), _resident(w.shape)],
        out_specs=tile(D),
        out_shape=jax.ShapeDtypeStruct(x.shape, F32),
        compiler_params=_params("parallel", "parallel"),
        name="out_proj",
    )(x, o, w.astype(BF16))


def kernel(x, a_norm, a_w_in, a_conv_w, a_log, a_dt_bias, a_out_norm, a_w_out, kv_norm, kv_w, k_norm,
           b_norm, b_w_q, b_q_norm, b_lambda, b_subln, b_w_out, f_norm, f_w_up, f_conv_w, f_conv_b,
           f_w_down):
    n_a = a_norm.shape[0]
    depth = f_norm.shape[0]
    groups = DIFF_HEADS * 2
    k_sh = v_sh = None
    for i in range(depth):
        if i < n_a:
            q, k, v, z, gcol, grow = _gdn_in(x, a_norm[i], a_w_in[i], a_conv_w[i], a_log[i], a_dt_bias[i])
            x = _gdn_chunk(x, q, k, v, z, gcol, grow, a_out_norm[i], a_w_out[i])
        else:
            if i == n_a:
                k_sh, vt_sh = _proj(x, kv_norm, kv_w, jnp.tile(k_norm, groups), tm=ATTN_BLOCK)
            j = i - n_a
            lam_init = 0.8 - 0.6 * math.exp(-0.3 * i)
            q_gain = jnp.tile(b_q_norm[j], groups) * (DIFF_DH ** -0.5 * LOG2E)
            (q,) = _proj(x, b_norm[j], b_w_q[j], q_gain, tm=ATTN_BLOCK)
            o = _diff_attention(q, k_sh, vt_sh, b_lambda[j], b_subln[j], lam_init)
            x = _out_proj(x, o, b_w_out[j])
        x = _conv_ffn(x, f_norm[i], f_w_up[i], f_conv_w[i], f_conv_b[i], f_w_down[i])
    return x
```

```python
import functools
import math

import jax
import jax.numpy as jnp
from jax import lax
from jax.experimental import pallas as pl
from jax.experimental.pallas import tpu as pltpu

F32 = jnp.float32
BF16 = jnp.bfloat16

EPS = 1e-6
LANES = 128
HALO = 16
GDN_HEADS = 8
GDN_DK = 128
GDN_CHUNK = 64
GDN_ROWS = 256
DIFF_HEADS = 8
DIFF_DH = 64
ATTN_BLOCK = 512
LOG2E = math.log2(math.e)
ONES_ROWS = 16
FF_COLS = 256
FFN_AHEAD = 2
VMEM_LIMIT = 56 * 1024 * 1024


def _params(*sem):
    return pltpu.CompilerParams(dimension_semantics=sem, vmem_limit_bytes=VMEM_LIMIT)


def _resident(shape):
    zeros = (0,) * len(shape)
    return pl.BlockSpec(shape, lambda *_: zeros, pipeline_mode=pl.Buffered(1))


def _rms(x, g):
    ms = jnp.mean(x * x, axis=-1, keepdims=True)
    return x * lax.rsqrt(ms + EPS) * g


def _sigmoid(x):
    return 1.0 / (1.0 + jnp.exp(-x))


def _silu(x):
    return x * _sigmoid(x)


def _dot(a, b):
    return jnp.dot(a.astype(BF16), b.astype(BF16), preferred_element_type=F32)


def _dot_nt(a, b):
    return lax.dot_general(a.astype(BF16), b.astype(BF16), (((1,), (1,)), ((), ())),
                           preferred_element_type=F32)


def _dot_tn(a, b):
    return lax.dot_general(a.astype(BF16), b.astype(BF16), (((0,), (0,)), ((), ())),
                           preferred_element_type=F32)


def _normed_with_halo(x_ref, xh_ref, g_ref):
    g = g_ref[...]
    h = _rms(x_ref[0], g).astype(BF16)
    keep = (pl.program_id(1) > 0).astype(F32)
    hh = (_rms(xh_ref[0], g) * keep).astype(BF16)
    return h, jnp.concatenate([hh, h], axis=0)


def _causal_taps(u, cw, taps):
    y = u[HALO:] * cw[taps - 1:taps]
    for d in range(1, taps):
        y = y + pltpu.roll(u, d, 0)[HALO:] * cw[taps - 1 - d:taps - d]
    return y


def _halo_specs(tm, d):
    tile = pl.BlockSpec((1, tm, d), lambda b, s: (b, s, 0))
    halo = pl.BlockSpec((1, HALO, d), lambda b, s: (b, jnp.maximum(s * (tm // HALO) - 1, 0), 0))
    return tile, halo


def _ffn_kernel(x_ref, xh_ref, g_ref, wg_ref, wu_ref, cwg_ref, cwu_ref, cbg_ref, cbu_ref, wd_ref, o_ref, u_ref,
                *, n_slabs, taps):
    _, hext = _normed_with_halo(x_ref, xh_ref, g_ref)
    tm = x_ref.shape[1]
    n_slots = u_ref.shape[0]

    def up_proj(c):
        u_ref[c % n_slots, 0] = _dot(hext, wg_ref[c])
        u_ref[c % n_slots, 1] = _dot(hext, wu_ref[c])

    def conv(c, part, cw_ref, cb_ref):
        y = cb_ref[c]
        for d in range(taps):
            y = y + u_ref[c % n_slots, part, HALO - d:HALO - d + tm, :] * cw_ref[c, taps - 1 - d:taps - d]
        return y

    acc = None
    for c in range(min(FFN_AHEAD, n_slabs)):
        up_proj(c)
    for c in range(n_slabs):
        if c + FFN_AHEAD < n_slabs:
            up_proj(c + FFN_AHEAD)
        d = _dot(_silu(conv(c, 0, cwg_ref, cbg_ref)) * conv(c, 1, cwu_ref, cbu_ref), wd_ref[c])
        acc = d if acc is None else acc + d
    o_ref[0] = x_ref[0] + acc


def _conv_ffn(x, g, w_up, conv_w, conv_b, w_down, *, tm=512):
    B, S, D = x.shape
    ff = w_down.shape[0]
    taps = conv_w.shape[0]
    n_slabs = ff // FF_COLS
    assert n_slabs * FF_COLS == ff and S % tm == 0

    def slabs(a):
        a = a.reshape(a.shape[0], 2, n_slabs, FF_COLS).transpose(1, 2, 0, 3)
        return a[0], a[1]

    wg, wu = slabs(w_up.astype(BF16))
    cwg, cwu = slabs(conv_w)
    cbg, cbu = slabs(conv_b[None, :])
    wd = w_down.astype(BF16).reshape(n_slabs, FF_COLS, D)
    tile, halo = _halo_specs(tm, D)
    return pl.pallas_call(
        functools.partial(_ffn_kernel, n_slabs=n_slabs, taps=taps),
        grid=(B, S // tm),
        in_specs=[tile, halo, _resident((1, D)),
                  _resident(wg.shape), _resident(wu.shape), _resident(cwg.shape), _resident(cwu.shape),
                  _resident(cbg.shape), _resident(cbu.shape), _resident(wd.shape)],
        out_specs=tile,
        out_shape=jax.ShapeDtypeStruct(x.shape, F32),
        scratch_shapes=[pltpu.VMEM((FFN_AHEAD + 1, 2, HALO + tm, FF_COLS), F32)],
        compiler_params=_params("parallel", "parallel"),
        name="conv_ffn",
    )(x, x, g[None, :], wg, wu, cwg, cwu, cbg, cbu, wd)


def _chunk_cumsum(g):
    row = lax.broadcasted_iota(jnp.int32, g.shape, 0) & (GDN_CHUNK - 1)
    step = 1
    while step < GDN_CHUNK:
        g = g + jnp.where(row >= step, pltpu.roll(g, step, 0), 0.0)
        step *= 2
    return g


def _gdn_in_kernel(x_ref, xh_ref, g_ref, wqkv_ref, cw_ref, wz_ref, wab_ref, alog_ref, dtb_ref,
                   q_ref, k_ref, v_ref, z_ref, gcol_ref, grow_ref, *, taps):
    h, hext = _normed_with_halo(x_ref, xh_ref, g_ref)
    qk_cols = GDN_HEADS * GDN_DK
    outs = (q_ref, k_ref, v_ref)
    for c in range(wqkv_ref.shape[1] // 256):
        cols = slice(c * 256, (c + 1) * 256)
        y = _silu(_causal_taps(_dot(hext, wqkv_ref[:, cols]), cw_ref[:, cols], taps))
        for half in range(2):
            col = c * 256 + half * LANES
            yy = y[:, half * LANES:(half + 1) * LANES]
            which, off = divmod(col, qk_cols)
            if which < 2:
                yy = yy * lax.rsqrt(jnp.sum(yy * yy, axis=-1, keepdims=True) + EPS)
            if which == 0:
                yy = yy * (GDN_DK ** -0.5)
            outs[which][0, :, off:off + LANES] = yy.astype(outs[which].dtype)
    for c in range(wz_ref.shape[1] // 256):
        cols = slice(c * 256, (c + 1) * 256)
        z_ref[0, :, cols] = _dot(h, wz_ref[:, cols]).astype(z_ref.dtype)
    ab = _dot(h, wab_ref[...])
    t = ab + dtb_ref[...]
    softplus = jnp.maximum(t, 0.0) + jnp.log1p(jnp.exp(-jnp.abs(t)))
    cum = _chunk_cumsum(-jnp.exp(alog_ref[...]) * softplus)
    lane = lax.broadcasted_iota(jnp.int32, ab.shape, 1)
    gcol_ref[0] = jnp.where(lane < GDN_HEADS, cum, _sigmoid(ab))
    grow_ref[0] = cum.T[:GDN_HEADS, :]


def _gdn_in(x, g, w_in, conv_w, a_log, dt_bias, *, tm=512, act_dtype=BF16):
    B, S, D = x.shape
    H = GDN_HEADS
    qk = H * GDN_DK
    nqkv = conv_w.shape[1]
    nz = nqkv - 2 * qk
    wqkv = w_in[:, :nqkv].astype(BF16)
    wz = w_in[:, nqkv:nqkv + nz].astype(BF16)
    wab = jnp.pad(w_in[:, nqkv + nz:], ((0, 0), (0, LANES - 2 * H))).astype(BF16)
    alog = jnp.pad(a_log, (0, LANES - H))[None, :]
    dtb = jnp.pad(dt_bias, (0, LANES - H))[None, :]
    tile, halo = _halo_specs(tm, D)
    act = lambda n: jax.ShapeDtypeStruct((B, S, n), act_dtype)
    act_spec = lambda n: pl.BlockSpec((1, tm, n), lambda b, s: (b, s, 0))
    return pl.pallas_call(
        functools.partial(_gdn_in_kernel, taps=conv_w.shape[0]),
        grid=(B, S // tm),
        in_specs=[tile, halo, _resident((1, D)), _resident(wqkv.shape), _resident(conv_w.shape),
                  _resident(wz.shape), _resident(wab.shape), _resident(alog.shape), _resident(dtb.shape)],
        out_specs=[act_spec(qk), act_spec(qk), act_spec(nz), act_spec(nz), act_spec(LANES),
                   pl.BlockSpec((1, H, tm), lambda b, s: (b, 0, s))],
        out_shape=[act(qk), act(qk), act(nz), act(nz),
                   jax.ShapeDtypeStruct((B, S, LANES), F32), jax.ShapeDtypeStruct((B, H, S), F32)],
        compiler_params=_params("parallel", "parallel"),
        name="gdn_in",
    )(x, x, g[None, :], wqkv, conv_w, wz, wab, alog, dtb)


def _each(f, *lists):
    return [f(*args) for args in zip(*lists)]


def _unit_lower_inverses(a_s, ri, ci):
    in16 = (ri >> 4) == (ci >> 4)
    in32 = (ri >> 5) == (ci >> 5)
    eye = jnp.where(ri == ci, 1.0, 0.0)
    d = _each(lambda a: jnp.where(in16, a, 0.0), a_s)
    d2 = _each(_dot, d, d)
    x = _each(lambda d_: eye - d_, d)
    d4 = _each(_dot, d2, d2)
    x = _each(lambda x_, p: x_ + _dot(x_, p), x, d2)
    d8 = _each(_dot, d4, d4)
    x = _each(lambda x_, p: x_ + _dot(x_, p), x, d4)
    x = _each(lambda x_, p: x_ + _dot(x_, p), x, d8)
    a32 = _each(lambda a: jnp.where(in32, a, 0.0), a_s)
    t = _each(lambda lo, d_, x_: _dot(lo - d_, x_), a32, d, x)
    x = _each(lambda x_, t_: x_ - _dot(x_, t_), x, t)
    t = _each(lambda a, lo, x_: _dot(a - lo, x_), a_s, a32, x)
    return _each(lambda x_, t_: x_ - _dot(x_, t_), x, t)


def _gdn_chunk_kernel(q_ref, k_ref, v_ref, z_ref, gcol_ref, grow_ref, x_ref, onorm_ref, wout_ref,
                      o_ref, state_ref):
    R, C = GDN_ROWS, GDN_CHUNK
    ri = lax.broadcasted_iota(jnp.int32, (R, R), 0)
    ci = lax.broadcasted_iota(jnp.int32, (R, R), 1)
    same = (ri >> 6) == (ci >> 6)
    causal = same & (ri >= ci)
    strict = same & (ri > ci)
    gcol = gcol_ref[0]

    @pl.when(pl.program_id(1) == 0)
    def _():
        state_ref[...] = jnp.zeros(state_ref.shape, F32)

    heads = list(range(GDN_HEADS))
    head_lanes = [slice(h * LANES, (h + 1) * LANES) for h in heads]
    q = [q_ref[0, :, ln].astype(F32) for ln in head_lanes]
    k = [k_ref[0, :, ln].astype(F32) for ln in head_lanes]
    v = [v_ref[0, :, ln].astype(F32) for ln in head_lanes]
    gc = [gcol[:, h:h + 1] for h in heads]
    beta = [gcol[:, GDN_HEADS + h:GDN_HEADS + h + 1] for h in heads]
    decay = [jnp.exp(jnp.where(causal, g - grow_ref[0, h:h + 1, :], -jnp.inf)) for g, h in zip(gc, heads)]
    kb = [k_.astype(BF16) for k_ in k]
    kk = _each(_dot_nt, kb, kb)
    a = _each(lambda kk_, d, b: jnp.where(strict, kk_ * d * b, 0.0), kk, decay, beta)
    tinv = _unit_lower_inverses(a, ri, ci)
    attn = _each(lambda q_, kb_, d: _dot_nt(q_, kb_) * d, q, kb, decay)
    egc = [jnp.exp(g) for g in gc]
    g_last = [jnp.concatenate([jnp.broadcast_to(g[c * C + C - 1:(c + 1) * C, :], (C, 1)) for c in range(R // C)],
                              axis=0) for g in gc]
    sol = _each(lambda t, v_, k_, b, e: _dot(t, jnp.concatenate([v_ * b, k_ * (b * e)], axis=1)),
                tinv, v, k, beta, egc)
    u = [s_[:, :LANES] for s_ in sol]
    w = [s_[:, LANES:] for s_ in sol]
    q_dec = _each(lambda q_, e: q_ * e, q, egc)
    k_dec = _each(lambda k_, gl, g: k_ * jnp.exp(gl - g), k, g_last, gc)

    state = [state_ref[h] for h in heads]
    corrected = [[] for _ in heads]
    from_state = [[] for _ in heads]
    for c in range(R // C):
        rows = slice(c * C, (c + 1) * C)
        r = _each(lambda w_, qd, st: _dot(jnp.concatenate([w_[rows], qd[rows]], axis=0), st), w, q_dec, state)
        uc = _each(lambda u_, r_: u_[rows] - r_[:C], u, r)
        for j in heads:
            corrected[j].append(uc[j])
            from_state[j].append(r[j][C:])
        state = _each(lambda st, gl, kd, uc_: st * jnp.exp(gl[c * C:c * C + 1, :]) + _dot_tn(kd[rows], uc_),
                      state, g_last, k_dec, uc)
    for h in heads:
        state_ref[h] = state[h]
    o = _each(lambda fs, at, co: jnp.concatenate(fs, axis=0) + _dot(at, jnp.concatenate(co, axis=0)),
              from_state, attn, corrected)
    o = _each(lambda o_, ln: _rms(o_, onorm_ref[...]) * _silu(z_ref[0, :, ln].astype(F32)), o, head_lanes)
    o_ref[0] = x_ref[0] + _dot(jnp.concatenate([o_.astype(BF16) for o_ in o], axis=1), wout_ref[...])


def _gdn_chunk(x, q, k, v, z, gcol, grow, out_norm, w_out):
    B, S, D = x.shape
    H, R = GDN_HEADS, GDN_ROWS
    dv = w_out.shape[0] // H
    row_block = lambda n: pl.BlockSpec((1, R, n), lambda b, s: (b, s, 0))
    return pl.pallas_call(
        _gdn_chunk_kernel,
        grid=(B, S // R),
        in_specs=[row_block(q.shape[2]), row_block(k.shape[2]), row_block(v.shape[2]), row_block(z.shape[2]),
                  row_block(LANES), pl.BlockSpec((1, H, R), lambda b, s: (b, 0, s)),
                  row_block(D), _resident((1, dv)), _resident(w_out.shape)],
        out_specs=row_block(D),
        out_shape=jax.ShapeDtypeStruct(x.shape, F32),
        scratch_shapes=[pltpu.VMEM((H, GDN_DK, dv), F32)],
        compiler_params=_params("parallel", "arbitrary"),
        name="gdn_chunk",
    )(q, k, v, z, gcol, grow, x, out_norm[None, :], w_out.astype(BF16))


def _proj_kernel(x_ref, g_ref, w_ref, gain_ref, o_ref, ot_ref=None, *, n_normed):
    h = _rms(x_ref[0], g_ref[...]).astype(BF16)
    width = o_ref.shape[2]
    for c in range(w_ref.shape[1] // 256):
        y = _dot(h, w_ref[:, c * 256:(c + 1) * 256])
        for half in range(2):
            col = c * 256 + half * LANES
            yy = y[:, half * LANES:(half + 1) * LANES]
            if col < n_normed:
                lane = lax.broadcasted_iota(jnp.int32, yy.shape, 1)
                low = lane < DIFF_DH
                sq = yy * yy
                ms = jnp.where(low,
                               jnp.sum(jnp.where(low, sq, 0.0), axis=-1, keepdims=True),
                               jnp.sum(jnp.where(low, 0.0, sq), axis=-1, keepdims=True)) * (1.0 / DIFF_DH)
                yy = yy * lax.rsqrt(ms + EPS) * gain_ref[:, col:col + LANES]
            if col < width:
                o_ref[0, :, col:col + LANES] = yy.astype(o_ref.dtype)
            else:
                head = (col - width) // LANES
                ot_ref[0, head, 0, :LANES, :] = yy.T.astype(ot_ref.dtype)
                ot_ref[0, head, 0, LANES:, :] = jnp.ones((ONES_ROWS, yy.shape[0]), ot_ref.dtype)


def _proj(x, g, w, gain, *, tm):
    B, S, D = x.shape
    width = gain.shape[0]
    heads_t = (w.shape[1] - width) // LANES
    tile = pl.BlockSpec((1, tm, D), lambda b, s: (b, s, 0))
    out_specs = [pl.BlockSpec((1, tm, width), lambda b, s: (b, s, 0))]
    out_shape = [jax.ShapeDtypeStruct((B, S, width), BF16)]
    if heads_t:
        rows = LANES + ONES_ROWS
        out_specs.append(pl.BlockSpec((1, heads_t, 1, rows, tm), lambda b, s: (b, 0, s, 0, 0)))
        out_shape.append(jax.ShapeDtypeStruct((B, heads_t, S // tm, rows, tm), BF16))
    return pl.pallas_call(
        functools.partial(_proj_kernel, n_normed=width),
        grid=(B, S // tm),
        in_specs=[tile, _resident((1, D)), _resident(w.shape), _resident((1, width))],
        out_specs=out_specs,
        out_shape=out_shape,
        compiler_params=_params("parallel", "parallel"),
        name="norm_proj",
    )(x, g[None, :], w.astype(BF16), gain[None, :])


def _attn_kernel(lam_ref, q_ref, k_ref, vt_ref, sub_ref, o_ref, m_ref, acc_ref, sa_ref, sb_ref, *, lam_init):
    i = pl.program_id(2)
    tq, dv = o_ref.shape[1:]
    tk = vt_ref.shape[4]
    assert tq == 2 * tk
    q = q_ref[0]
    lane = lax.broadcasted_iota(jnp.int32, q.shape, 1)
    qs = (jnp.where(lane < DIFF_DH, q, jnp.zeros_like(q)), jnp.where(lane < DIFF_DH, jnp.zeros_like(q), q))
    m_ref[...] = jnp.full(m_ref.shape, -jnp.inf, F32)
    acc_ref[...] = jnp.zeros(acc_ref.shape, F32)

    def scores(j, dst_ref, q0=0):
        kb = k_ref[0, pl.ds(pl.multiple_of(j * tk, tk), tk), :]
        for c in range(2):
            dst_ref[c, :, q0:] = _dot_nt(kb, qs[c][q0:])

    def consume(src_ref, j, q0=0, q1=tq, masked=False):
        vt = vt_ref[0, 0, j]
        for c in range(2):
            sc = src_ref[c, :, q0:q1]
            if masked:
                key = lax.broadcasted_iota(jnp.int32, sc.shape, 0)
                qry = lax.broadcasted_iota(jnp.int32, sc.shape, 1)
                sc = jnp.where(key <= qry, sc, -jnp.inf)
            m_prev = m_ref[c, :, q0:q1]
            m_new = jnp.maximum(m_prev, jnp.max(sc, axis=0, keepdims=True))
            p = jnp.exp2(sc - m_new)
            acc_ref[c, :, q0:q1] = jnp.exp2(m_prev - m_new) * acc_ref[c, :, q0:q1] + _dot(vt, p)
            m_ref[c, :, q0:q1] = m_new

    scores(0, sa_ref)

    def two_blocks(pair, carry):
        j = 2 * pair
        scores(j + 1, sb_ref)
        consume(sa_ref, j)
        scores(j + 2, sa_ref)
        consume(sb_ref, j + 1)
        return carry

    lax.fori_loop(0, i, two_blocks, 0)
    scores(2 * i + 1, sb_ref, tk)
    consume(sa_ref, 2 * i, 0, tk, masked=True)
    consume(sa_ref, 2 * i, tk, tq)
    consume(sb_ref, 2 * i + 1, tk, tq, masked=True)

    lam = lam_ref[...]
    lam_full = (jnp.exp(jnp.sum(lam[0:1] * lam[1:2], axis=-1, keepdims=True))
                - jnp.exp(jnp.sum(lam[2:3] * lam[3:4], axis=-1, keepdims=True)) + lam_init)
    o = (acc_ref[0, :dv] / acc_ref[0, dv:dv + 1] - lam_full * (acc_ref[1, :dv] / acc_ref[1, dv:dv + 1]))
    ms = jnp.mean(o * o, axis=0, keepdims=True)
    o = o * lax.rsqrt(ms + EPS) * (sub_ref[...] * (1.0 - lam_init))
    o_ref[0] = o.T.astype(o_ref.dtype)


def _diff_attention(q, k, vt, lam, subln, lam_init):
    B, S, _ = q.shape
    H = DIFF_HEADS
    nk, rows, tk = vt.shape[2:]
    tq = 2 * tk
    dv = subln.shape[0]
    k_spec = pl.BlockSpec((1, S, LANES), lambda b, h, i: (b, 0, h))
    vt_spec = pl.BlockSpec((1, 1, nk, rows, tk), lambda b, h, i: (b, h, 0, 0, 0))
    q_spec = pl.BlockSpec((1, tq, LANES), lambda b, h, i: (b, i, h))
    return pl.pallas_call(
        functools.partial(_attn_kernel, lam_init=lam_init),
        grid=(B, H, S // tq),
        in_specs=[_resident(lam.shape), q_spec, k_spec, vt_spec, _resident((dv, 1))],
        out_specs=q_spec,
        out_shape=jax.ShapeDtypeStruct(q.shape, BF16),
        scratch_shapes=[pltpu.VMEM((2, 1, tq), F32), pltpu.VMEM((2, rows, tq), F32),
                        pltpu.VMEM((2, tk, tq), F32), pltpu.VMEM((2, tk, tq), F32)],
        compiler_params=_params("parallel", "parallel", "parallel"),
        name="diff_attention",
    )(lam, q, k, vt, subln[:, None])


def _out_proj_kernel(x_ref, o_ref, w_ref, y_ref):
    y_ref[0] = x_ref[0] + _dot(o_ref[0], w_ref[...])


def _out_proj(x, o, w, *, tm=512):
    B, S, D = x.shape
    tile = lambda n: pl.BlockSpec((1, tm, n), lambda b, s: (b, s, 0))
    return pl.pallas_call(
        _out_proj_kernel,
        grid=(B, S // tm),
        in_specs=[tile(D), tile(o.shape[2]), _resident(w.shape)],
        out_specs=tile(D),
        out_shape=jax.ShapeDtypeStruct(x.shape, F32),
        compiler_params=_params("parallel", "parallel"),
        name="out_proj",
    )(x, o, w.astype(BF16))


def kernel(x, a_norm, a_w_in, a_conv_w, a_log, a_dt_bias, a_out_norm, a_w_out, kv_norm, kv_w, k_norm,
           b_norm, b_w_q, b_q_norm, b_lambda, b_subln, b_w_out, f_norm, f_w_up, f_conv_w, f_conv_b,
           f_w_down):
    n_a = a_norm.shape[0]
    depth = f_norm.shape[0]
    groups = DIFF_HEADS * 2
    k_sh = v_sh = None
    for i in range(depth):
        if i < n_a:
            q, k, v, z, gcol, grow = _gdn_in(x, a_norm[i], a_w_in[i], a_conv_w[i], a_log[i], a_dt_bias[i])
            x = _gdn_chunk(x, q, k, v, z, gcol, grow, a_out_norm[i], a_w_out[i])
        else:
            if i == n_a:
                k_sh, vt_sh = _proj(x, kv_norm, kv_w, jnp.tile(k_norm, groups), tm=ATTN_BLOCK)
            j = i - n_a
            lam_init = 0.8 - 0.6 * math.exp(-0.3 * i)
            q_gain = jnp.tile(b_q_norm[j], groups) * (DIFF_DH ** -0.5 * LOG2E)
            (q,) = _proj(x, b_norm[j], b_w_q[j], q_gain, tm=ATTN_BLOCK)
            o = _diff_attention(q, k_sh, vt_sh, b_lambda[j], b_subln[j], lam_init)
            x = _out_proj(x, o, b_w_out[j])
        x = _conv_ffn(x, f_norm[i], f_w_up[i], f_conv_w[i], f_conv_b[i], f_w_down[i])
    return x
```

```python
import functools
import math

import jax
import jax.numpy as jnp
from jax import lax
from jax.experimental import pallas as pl
from jax.experimental.pallas import tpu as pltpu

F32 = jnp.float32
BF16 = jnp.bfloat16

EPS = 1e-6
LANES = 128
HALO = 16
GDN_HEADS = 8
GDN_DK = 128
GDN_CHUNK = 64
GDN_ROWS = 256
DIFF_HEADS = 8
DIFF_DH = 64
ATTN_BLOCK = 512
ATTN_QUERY_BLOCKS = 4
LOG2E = math.log2(math.e)
ONES_ROWS = 16
FF_COLS = 256
FFN_AHEAD = 2
VMEM_LIMIT = 56 * 1024 * 1024


def _params(*sem):
    return pltpu.CompilerParams(dimension_semantics=sem, vmem_limit_bytes=VMEM_LIMIT)


def _resident(shape):
    zeros = (0,) * len(shape)
    return pl.BlockSpec(shape, lambda *_: zeros, pipeline_mode=pl.Buffered(1))


def _rms(x, g):
    ms = jnp.mean(x * x, axis=-1, keepdims=True)
    return x * lax.rsqrt(ms + EPS) * g


def _sigmoid(x):
    return 1.0 / (1.0 + jnp.exp(-x))


def _silu(x):
    return x * _sigmoid(x)


def _dot(a, b):
    return jnp.dot(a.astype(BF16), b.astype(BF16), preferred_element_type=F32)


def _dot_nt(a, b):
    return lax.dot_general(a.astype(BF16), b.astype(BF16), (((1,), (1,)), ((), ())),
                           preferred_element_type=F32)


def _dot_tn(a, b):
    return lax.dot_general(a.astype(BF16), b.astype(BF16), (((0,), (0,)), ((), ())),
                           preferred_element_type=F32)


def _normed_with_halo(x_ref, xh_ref, g_ref):
    g = g_ref[...]
    h = _rms(x_ref[0], g).astype(BF16)
    keep = (pl.program_id(1) > 0).astype(F32)
    hh = (_rms(xh_ref[0], g) * keep).astype(BF16)
    return h, jnp.concatenate([hh, h], axis=0)


def _causal_taps(u, cw, taps):
    y = u[HALO:] * cw[taps - 1:taps]
    for d in range(1, taps):
        y = y + pltpu.roll(u, d, 0)[HALO:] * cw[taps - 1 - d:taps - d]
    return y


def _halo_specs(tm, d):
    tile = pl.BlockSpec((1, tm, d), lambda b, s: (b, s, 0))
    halo = pl.BlockSpec((1, HALO, d), lambda b, s: (b, jnp.maximum(s * (tm // HALO) - 1, 0), 0))
    return tile, halo


def _ffn_kernel(x_ref, xh_ref, g_ref, wup_ref, cw_ref, cb_ref, wd_ref, o_ref, u_ref):
    _, hext = _normed_with_halo(x_ref, xh_ref, g_ref)
    tm = x_ref.shape[1]
    n_slots = u_ref.shape[0]
    taps = cw_ref.shape[0]
    ff = wd_ref.shape[0]
    n_slabs = ff // FF_COLS

    def cols(c, part):
        return slice(part * ff + c * FF_COLS, part * ff + (c + 1) * FF_COLS)

    def up_proj(c):
        for part in range(2):
            u_ref[c % n_slots, part] = _dot(hext, wup_ref[:, cols(c, part)])

    def conv(c, part):
        y = cb_ref[:, cols(c, part)]
        for d in range(taps):
            y = y + (u_ref[c % n_slots, part, HALO - d:HALO - d + tm, :]
                     * cw_ref[taps - 1 - d:taps - d, cols(c, part)])
        return y

    acc = None
    for c in range(min(FFN_AHEAD, n_slabs)):
        up_proj(c)
    for c in range(n_slabs):
        if c + FFN_AHEAD < n_slabs:
            up_proj(c + FFN_AHEAD)
        d = _dot(_silu(conv(c, 0)) * conv(c, 1), wd_ref[c * FF_COLS:(c + 1) * FF_COLS, :])
        acc = d if acc is None else acc + d
    o_ref[0] = x_ref[0] + acc


def _conv_ffn(x, g, w_up, conv_w, conv_b, w_down, *, tm=512):
    B, S, D = x.shape
    ff = w_down.shape[0]
    assert ff % FF_COLS == 0 and S % tm == 0
    tile, halo = _halo_specs(tm, D)
    return pl.pallas_call(
        _ffn_kernel,
        grid=(B, S // tm),
        in_specs=[tile, halo, _resident((1, D)), _resident(w_up.shape), _resident(conv_w.shape),
                  _resident((1, 2 * ff)), _resident(w_down.shape)],
        out_specs=tile,
        out_shape=jax.ShapeDtypeStruct(x.shape, F32),
        scratch_shapes=[pltpu.VMEM((FFN_AHEAD + 1, 2, HALO + tm, FF_COLS), F32)],
        compiler_params=_params("parallel", "parallel"),
        name="conv_ffn",
    )(x, x, g[None, :], w_up.astype(BF16), conv_w, conv_b[None, :], w_down.astype(BF16))


def _chunk_cumsum(g):
    row = lax.broadcasted_iota(jnp.int32, g.shape, 0) & (GDN_CHUNK - 1)
    step = 1
    while step < GDN_CHUNK:
        g = g + jnp.where(row >= step, pltpu.roll(g, step, 0), 0.0)
        step *= 2
    return g


def _gdn_in_kernel(x_ref, xh_ref, g_ref, wqkv_ref, cw_ref, wz_ref, wab_ref, alog_ref, dtb_ref,
                   q_ref, k_ref, v_ref, z_ref, gcol_ref, grow_ref, *, taps):
    h, hext = _normed_with_halo(x_ref, xh_ref, g_ref)
    qk_cols = GDN_HEADS * GDN_DK
    outs = (q_ref, k_ref, v_ref)
    for c in range(wqkv_ref.shape[1] // 256):
        cols = slice(c * 256, (c + 1) * 256)
        y = _silu(_causal_taps(_dot(hext, wqkv_ref[:, cols]), cw_ref[:, cols], taps))
        for half in range(2):
            col = c * 256 + half * LANES
            yy = y[:, half * LANES:(half + 1) * LANES]
            which, off = divmod(col, qk_cols)
            if which < 2:
                yy = yy * lax.rsqrt(jnp.sum(yy * yy, axis=-1, keepdims=True) + EPS)
            if which == 0:
                yy = yy * (GDN_DK ** -0.5)
            outs[which][0, :, off:off + LANES] = yy.astype(outs[which].dtype)
    for c in range(wz_ref.shape[1] // 256):
        cols = slice(c * 256, (c + 1) * 256)
        z_ref[0, :, cols] = _dot(h, wz_ref[:, cols]).astype(z_ref.dtype)
    ab = _dot(h, wab_ref[...])
    t = ab + dtb_ref[...]
    softplus = jnp.maximum(t, 0.0) + jnp.log1p(jnp.exp(-jnp.abs(t)))
    cum = _chunk_cumsum(-jnp.exp(alog_ref[...]) * softplus)
    lane = lax.broadcasted_iota(jnp.int32, ab.shape, 1)
    gcol_ref[0] = jnp.where(lane < GDN_HEADS, cum, _sigmoid(ab))
    grow_ref[0] = cum.T[:GDN_HEADS, :]


def _gdn_in(x, g, w_in, conv_w, a_log, dt_bias, *, tm=512, act_dtype=BF16):
    B, S, D = x.shape
    H = GDN_HEADS
    qk = H * GDN_DK
    nqkv = conv_w.shape[1]
    nz = nqkv - 2 * qk
    wqkv = w_in[:, :nqkv].astype(BF16)
    wz = w_in[:, nqkv:nqkv + nz].astype(BF16)
    wab = jnp.pad(w_in[:, nqkv + nz:], ((0, 0), (0, LANES - 2 * H))).astype(BF16)
    alog = jnp.pad(a_log, (0, LANES - H))[None, :]
    dtb = jnp.pad(dt_bias, (0, LANES - H))[None, :]
    tile, halo = _halo_specs(tm, D)
    act = lambda n: jax.ShapeDtypeStruct((B, S, n), act_dtype)
    act_spec = lambda n: pl.BlockSpec((1, tm, n), lambda b, s: (b, s, 0))
    return pl.pallas_call(
        functools.partial(_gdn_in_kernel, taps=conv_w.shape[0]),
        grid=(B, S // tm),
        in_specs=[tile, halo, _resident((1, D)), _resident(wqkv.shape), _resident(conv_w.shape),
                  _resident(wz.shape), _resident(wab.shape), _resident(alog.shape), _resident(dtb.shape)],
        out_specs=[act_spec(qk), act_spec(qk), act_spec(nz), act_spec(nz), act_spec(LANES),
                   pl.BlockSpec((1, H, tm), lambda b, s: (b, 0, s))],
        out_shape=[act(qk), act(qk), act(nz), act(nz),
                   jax.ShapeDtypeStruct((B, S, LANES), F32), jax.ShapeDtypeStruct((B, H, S), F32)],
        compiler_params=_params("parallel", "parallel"),
        name="gdn_in",
    )(x, x, g[None, :], wqkv, conv_w, wz, wab, alog, dtb)


def _each(f, *lists):
    return [f(*args) for args in zip(*lists)]


def _unit_lower_inverses(a_s, ri, ci):
    in16 = (ri >> 4) == (ci >> 4)
    in32 = (ri >> 5) == (ci >> 5)
    eye = jnp.where(ri == ci, 1.0, 0.0)
    d = _each(lambda a: jnp.where(in16, a, 0.0), a_s)
    d2 = _each(_dot, d, d)
    x = _each(lambda d_: eye - d_, d)
    d4 = _each(_dot, d2, d2)
    x = _each(lambda x_, p: x_ + _dot(x_, p), x, d2)
    d8 = _each(_dot, d4, d4)
    x = _each(lambda x_, p: x_ + _dot(x_, p), x, d4)
    x = _each(lambda x_, p: x_ + _dot(x_, p), x, d8)
    a32 = _each(lambda a: jnp.where(in32, a, 0.0), a_s)
    t = _each(lambda lo, d_, x_: _dot(lo - d_, x_), a32, d, x)
    x = _each(lambda x_, t_: x_ - _dot(x_, t_), x, t)
    t = _each(lambda a, lo, x_: _dot(a - lo, x_), a_s, a32, x)
    return _each(lambda x_, t_: x_ - _dot(x_, t_), x, t)


def _gdn_chunk_kernel(q_ref, k_ref, v_ref, z_ref, gcol_ref, grow_ref, x_ref, onorm_ref, wout_ref,
                      o_ref, state_ref):
    R, C = GDN_ROWS, GDN_CHUNK
    ri = lax.broadcasted_iota(jnp.int32, (R, R), 0)
    ci = lax.broadcasted_iota(jnp.int32, (R, R), 1)
    same = (ri >> 6) == (ci >> 6)
    causal = same & (ri >= ci)
    strict = same & (ri > ci)
    gcol = gcol_ref[0]

    @pl.when(pl.program_id(1) == 0)
    def _():
        state_ref[...] = jnp.zeros(state_ref.shape, F32)

    heads = list(range(GDN_HEADS))
    head_lanes = [slice(h * LANES, (h + 1) * LANES) for h in heads]
    q = [q_ref[0, :, ln].astype(F32) for ln in head_lanes]
    k = [k_ref[0, :, ln].astype(F32) for ln in head_lanes]
    v = [v_ref[0, :, ln].astype(F32) for ln in head_lanes]
    gc = [gcol[:, h:h + 1] for h in heads]
    beta = [gcol[:, GDN_HEADS + h:GDN_HEADS + h + 1] for h in heads]
    decay = [jnp.exp(jnp.where(causal, g - grow_ref[0, h:h + 1, :], -jnp.inf)) for g, h in zip(gc, heads)]
    kb = [k_.astype(BF16) for k_ in k]
    kk = _each(_dot_nt, kb, kb)
    a = _each(lambda kk_, d, b: jnp.where(strict, kk_ * d * b, 0.0), kk, decay, beta)
    tinv = _unit_lower_inverses(a, ri, ci)
    attn = _each(lambda q_, kb_, d: _dot_nt(q_, kb_) * d, q, kb, decay)
    egc = [jnp.exp(g) for g in gc]
    g_last = [jnp.concatenate([jnp.broadcast_to(g[c * C + C - 1:(c + 1) * C, :], (C, 1)) for c in range(R // C)],
                              axis=0) for g in gc]
    sol = _each(lambda t, v_, k_, b, e: _dot(t, jnp.concatenate([v_ * b, k_ * (b * e)], axis=1)),
                tinv, v, k, beta, egc)
    u = [s_[:, :LANES] for s_ in sol]
    w = [s_[:, LANES:] for s_ in sol]
    q_dec = _each(lambda q_, e: q_ * e, q, egc)
    k_dec = _each(lambda k_, gl, g: k_ * jnp.exp(gl - g), k, g_last, gc)

    state = [state_ref[h] for h in heads]
    corrected = [[] for _ in heads]
    from_state = [[] for _ in heads]
    for c in range(R // C):
        rows = slice(c * C, (c + 1) * C)
        r = _each(lambda w_, qd, st: _dot(jnp.concatenate([w_[rows], qd[rows]], axis=0), st), w, q_dec, state)
        uc = _each(lambda u_, r_: u_[rows] - r_[:C], u, r)
        for j in heads:
            corrected[j].append(uc[j])
            from_state[j].append(r[j][C:])
        state = _each(lambda st, gl, kd, uc_: st * jnp.exp(gl[c * C:c * C + 1, :]) + _dot_tn(kd[rows], uc_),
                      state, g_last, k_dec, uc)
    for h in heads:
        state_ref[h] = state[h]
    o = _each(lambda fs, at, co: jnp.concatenate(fs, axis=0) + _dot(at, jnp.concatenate(co, axis=0)),
              from_state, attn, corrected)
    o = _each(lambda o_, ln: _rms(o_, onorm_ref[...]) * _silu(z_ref[0, :, ln].astype(F32)), o, head_lanes)
    o_ref[0] = x_ref[0] + _dot(jnp.concatenate([o_.astype(BF16) for o_ in o], axis=1), wout_ref[...])


def _gdn_chunk(x, q, k, v, z, gcol, grow, out_norm, w_out):
    B, S, D = x.shape
    H, R = GDN_HEADS, GDN_ROWS
    dv = w_out.shape[0] // H
    row_block = lambda n: pl.BlockSpec((1, R, n), lambda b, s: (b, s, 0))
    return pl.pallas_call(
        _gdn_chunk_kernel,
        grid=(B, S // R),
        in_specs=[row_block(q.shape[2]), row_block(k.shape[2]), row_block(v.shape[2]), row_block(z.shape[2]),
                  row_block(LANES), pl.BlockSpec((1, H, R), lambda b, s: (b, 0, s)),
                  row_block(D), _resident((1, dv)), _resident(w_out.shape)],
        out_specs=row_block(D),
        out_shape=jax.ShapeDtypeStruct(x.shape, F32),
        scratch_shapes=[pltpu.VMEM((H, GDN_DK, dv), F32)],
        compiler_params=_params("parallel", "arbitrary"),
        name="gdn_chunk",
    )(q, k, v, z, gcol, grow, x, out_norm[None, :], w_out.astype(BF16))


def _proj_kernel(x_ref, g_ref, w_ref, gain_ref, o_ref, ot_ref=None, *, n_normed):
    h = _rms(x_ref[0], g_ref[...]).astype(BF16)
    width = o_ref.shape[2]
    for c in range(w_ref.shape[1] // 256):
        y = _dot(h, w_ref[:, c * 256:(c + 1) * 256])
        for half in range(2):
            col = c * 256 + half * LANES
            yy = y[:, half * LANES:(half + 1) * LANES]
            if col < n_normed:
                lane = lax.broadcasted_iota(jnp.int32, yy.shape, 1)
                low = lane < DIFF_DH
                sq = yy * yy
                ms = jnp.where(low,
                               jnp.sum(jnp.where(low, sq, 0.0), axis=-1, keepdims=True),
                               jnp.sum(jnp.where(low, 0.0, sq), axis=-1, keepdims=True)) * (1.0 / DIFF_DH)
                yy = yy * lax.rsqrt(ms + EPS) * gain_ref[:, col:col + LANES]
            if col < width:
                o_ref[0, :, col:col + LANES] = yy.astype(o_ref.dtype)
            else:
                head = (col - width) // LANES
                ot_ref[0, head, 0, :LANES, :] = yy.T.astype(ot_ref.dtype)
                ot_ref[0, head, 0, LANES:, :] = jnp.ones((ONES_ROWS, yy.shape[0]), ot_ref.dtype)


def _proj(x, g, w, gain, *, tm):
    B, S, D = x.shape
    width = gain.shape[0]
    heads_t = (w.shape[1] - width) // LANES
    tile = pl.BlockSpec((1, tm, D), lambda b, s: (b, s, 0))
    out_specs = [pl.BlockSpec((1, tm, width), lambda b, s: (b, s, 0))]
    out_shape = [jax.ShapeDtypeStruct((B, S, width), BF16)]
    if heads_t:
        rows = LANES + ONES_ROWS
        out_specs.append(pl.BlockSpec((1, heads_t, 1, rows, tm), lambda b, s: (b, 0, s, 0, 0)))
        out_shape.append(jax.ShapeDtypeStruct((B, heads_t, S // tm, rows, tm), BF16))
    return pl.pallas_call(
        functools.partial(_proj_kernel, n_normed=width),
        grid=(B, S // tm),
        in_specs=[tile, _resident((1, D)), _resident(w.shape), _resident((1, width))],
        out_specs=out_specs,
        out_shape=out_shape,
        compiler_params=_params("parallel", "parallel"),
        name="norm_proj",
    )(x, g[None, :], w.astype(BF16), gain[None, :])


def _attn_kernel(lam_ref, q_ref, k_ref, vt_ref, sub_ref, o_ref, m_ref, acc_ref, sa_ref, sb_ref, *, lam_init):
    i = pl.program_id(2)
    tq, dv = o_ref.shape[1:]
    tk = vt_ref.shape[4]
    n = tq // tk
    assert n % 2 == 0
    q = q_ref[0]
    lane = lax.broadcasted_iota(jnp.int32, q.shape, 1)
    qs = (jnp.where(lane < DIFF_DH, q, jnp.zeros_like(q)), jnp.where(lane < DIFF_DH, jnp.zeros_like(q), q))
    m_ref[...] = jnp.full(m_ref.shape, -jnp.inf, F32)
    acc_ref[...] = jnp.zeros(acc_ref.shape, F32)

    def scores(j, dst_ref, q0=0):
        kb = k_ref[0, pl.ds(pl.multiple_of(j * tk, tk), tk), :]
        for c in range(2):
            dst_ref[c, :, q0:] = _dot_nt(kb, qs[c][q0:])

    def consume(src_ref, j, q0=0, q1=tq, masked=False):
        vt = vt_ref[0, 0, j]
        for c in range(2):
            sc = src_ref[c, :, q0:q1]
            if masked:
                key = lax.broadcasted_iota(jnp.int32, sc.shape, 0)
                qry = lax.broadcasted_iota(jnp.int32, sc.shape, 1)
                sc = jnp.where(key <= qry, sc, -jnp.inf)
            m_prev = m_ref[c, :, q0:q1]
            m_new = jnp.maximum(m_prev, jnp.max(sc, axis=0, keepdims=True))
            p = jnp.exp2(sc - m_new)
            acc_ref[c, :, q0:q1] = jnp.exp2(m_prev - m_new) * acc_ref[c, :, q0:q1] + _dot(vt, p)
            m_ref[c, :, q0:q1] = m_new

    scores(0, sa_ref)

    def two_blocks(pair, carry):
        j = 2 * pair
        scores(j + 1, sb_ref)
        consume(sa_ref, j)
        scores(j + 2, sa_ref)
        consume(sb_ref, j + 1)
        return carry

    lax.fori_loop(0, i * (n // 2), two_blocks, 0)
    bufs = (sa_ref, sb_ref)
    for d in range(n):
        if d + 1 < n:
            scores(n * i + d + 1, bufs[(d + 1) % 2], (d + 1) * tk)
        consume(bufs[d % 2], n * i + d, d * tk, (d + 1) * tk, masked=True)
        if d + 1 < n:
            consume(bufs[d % 2], n * i + d, (d + 1) * tk, tq)

    lam = lam_ref[...]
    lam_full = (jnp.exp(jnp.sum(lam[0:1] * lam[1:2], axis=-1, keepdims=True))
                - jnp.exp(jnp.sum(lam[2:3] * lam[3:4], axis=-1, keepdims=True)) + lam_init)
    o = (acc_ref[0, :dv] / acc_ref[0, dv:dv + 1] - lam_full * (acc_ref[1, :dv] / acc_ref[1, dv:dv + 1]))
    ms = jnp.mean(o * o, axis=0, keepdims=True)
    o = o * lax.rsqrt(ms + EPS) * (sub_ref[...] * (1.0 - lam_init))
    o_ref[0] = o.T.astype(o_ref.dtype)


def _diff_attention(q, k, vt, lam, subln, lam_init):
    B, S, _ = q.shape
    H = DIFF_HEADS
    nk, rows, tk = vt.shape[2:]
    tq = ATTN_QUERY_BLOCKS * tk
    dv = subln.shape[0]
    k_spec = pl.BlockSpec((1, S, LANES), lambda b, h, i: (b, 0, h))
    vt_spec = pl.BlockSpec((1, 1, nk, rows, tk), lambda b, h, i: (b, h, 0, 0, 0))
    q_spec = pl.BlockSpec((1, tq, LANES), lambda b, h, i: (b, i, h))
    return pl.pallas_call(
        functools.partial(_attn_kernel, lam_init=lam_init),
        grid=(B, H, S // tq),
        in_specs=[_resident(lam.shape), q_spec, k_spec, vt_spec, _resident((dv, 1))],
        out_specs=q_spec,
        out_shape=jax.ShapeDtypeStruct(q.shape, BF16),
        scratch_shapes=[pltpu.VMEM((2, 1, tq), F32), pltpu.VMEM((2, rows, tq), F32),
                        pltpu.VMEM((2, tk, tq), F32), pltpu.VMEM((2, tk, tq), F32)],
        compiler_params=_params("parallel", "parallel", "parallel"),
        name="diff_attention",
    )(lam, q, k, vt, subln[:, None])


def _out_proj_kernel(x_ref, o_ref, w_ref, y_ref):
    y_ref[0] = x_ref[0] + _dot(o_ref[0], w_ref[...])


def _out_proj(x, o, w, *, tm=512):
    B, S, D = x.shape
    tile = lambda n: pl.BlockSpec((1, tm, n), lambda b, s: (b, s, 0))
    return pl.pallas_call(
        _out_proj_kernel,
        grid=(B, S // tm),
        in_specs=[tile(D), tile(o.shape[2]), _resident(w.shape)],
        out_specs=tile(D),
        out_shape=jax.ShapeDtypeStruct(x.shape, F32),
        compiler_params=_params("parallel", "parallel"),
        name="out_proj",
    )(x, o, w.astype(BF16))


def kernel(x, a_norm, a_w_in, a_conv_w, a_log, a_dt_bias, a_out_norm, a_w_out, kv_norm, kv_w, k_norm,
           b_norm, b_w_q, b_q_norm, b_lambda, b_subln, b_w_out, f_norm, f_w_up, f_conv_w, f_conv_b,
           f_w_down):
    n_a = a_norm.shape[0]
    depth = f_norm.shape[0]
    groups = DIFF_HEADS * 2
    a_w_in, a_w_out, kv_w, b_w_q, b_w_out, f_w_up, f_w_down = (
        w.astype(BF16) for w in (a_w_in, a_w_out, kv_w, b_w_q, b_w_out, f_w_up, f_w_down))
    k_sh = vt_sh = None
    for i in range(depth):
        if i < n_a:
            q, k, v, z, gcol, grow = _gdn_in(x, a_norm[i], a_w_in[i], a_conv_w[i], a_log[i], a_dt_bias[i])
            x = _gdn_chunk(x, q, k, v, z, gcol, grow, a_out_norm[i], a_w_out[i])
        else:
            if i == n_a:
                k_sh, vt_sh = _proj(x, kv_norm, kv_w, jnp.tile(k_norm, groups), tm=ATTN_BLOCK)
            j = i - n_a
            lam_init = 0.8 - 0.6 * math.exp(-0.3 * i)
            q_gain = jnp.tile(b_q_norm[j], groups) * (DIFF_DH ** -0.5 * LOG2E)
            (q,) = _proj(x, b_norm[j], b_w_q[j], q_gain, tm=ATTN_BLOCK)
            o = _diff_attention(q, k_sh, vt_sh, b_lambda[j], b_subln[j], lam_init)
            x = _out_proj(x, o, b_w_out[j])
        x = _conv_ffn(x, f_norm[i], f_w_up[i], f_conv_w[i], f_conv_b[i], f_w_down[i])
    return x
```

```python
import functools
import math

import jax
import jax.numpy as jnp
from jax import lax
from jax.experimental import pallas as pl
from jax.experimental.pallas import tpu as pltpu

F32 = jnp.float32
BF16 = jnp.bfloat16

EPS = 1e-6
LANES = 128
MXU_COLS = 256
HALO = 16
GDN_HEADS = 8
GDN_DK = 128
GDN_CHUNK = 64
GDN_ROWS = 256
DIFF_HEADS = 8
DIFF_DH = 64
ATTN_BLOCK = 512
ATTN_QUERY_BLOCKS = 4
LOG2E = math.log2(math.e)
ONES_ROWS = 16
FF_COLS = MXU_COLS
FFN_AHEAD = 2
SOLVE_BLOCK = 16
VMEM_LIMIT = 56 * 1024 * 1024


def _params(*sem):
    return pltpu.CompilerParams(dimension_semantics=sem, vmem_limit_bytes=VMEM_LIMIT)


def _resident(shape):
    zeros = (0,) * len(shape)
    return pl.BlockSpec(shape, lambda *_: zeros, pipeline_mode=pl.Buffered(1))


def _rms(x, g):
    ms = jnp.mean(x * x, axis=-1, keepdims=True)
    return x * lax.rsqrt(ms + EPS) * g


def _sigmoid(x):
    return 1.0 / (1.0 + jnp.exp(-x))


def _silu(x):
    half = 0.5 * x
    return half * (1.0 + jnp.tanh(half))


def _dot(a, b):
    return jnp.dot(a.astype(BF16), b.astype(BF16), preferred_element_type=F32)


def _dot_nt(a, b):
    return lax.dot_general(a.astype(BF16), b.astype(BF16), (((1,), (1,)), ((), ())),
                           preferred_element_type=F32)


def _dot_tn(a, b):
    return lax.dot_general(a.astype(BF16), b.astype(BF16), (((0,), (0,)), ((), ())),
                           preferred_element_type=F32)


def _normed_with_halo(x_ref, xh_ref, g_ref):
    g = g_ref[...]
    h = _rms(x_ref[0], g).astype(BF16)
    keep = (pl.program_id(1) > 0).astype(F32)
    hh = (_rms(xh_ref[0], g) * keep).astype(BF16)
    return h, jnp.concatenate([hh, h], axis=0)


def _causal_taps(rows, cw, tm, bias=None):
    taps = cw.shape[0]
    y = bias
    for d in range(taps):
        term = rows(HALO - d, HALO - d + tm) * cw[taps - 1 - d:taps - d]
        y = term if y is None else y + term
    return y


def _halo_specs(tm, d):
    tile = pl.BlockSpec((1, tm, d), lambda b, s: (b, s, 0))
    halo = pl.BlockSpec((1, HALO, d), lambda b, s: (b, jnp.maximum(s * (tm // HALO) - 1, 0), 0))
    return tile, halo


def _ffn_kernel(x_ref, xh_ref, g_ref, wup_ref, cw_ref, cb_ref, wd_ref, o_ref, u_ref):
    _, hext = _normed_with_halo(x_ref, xh_ref, g_ref)
    tm = x_ref.shape[1]
    n_slots = u_ref.shape[0]
    ff = wd_ref.shape[0]
    n_slabs = ff // FF_COLS

    def cols(c, part):
        return slice(part * ff + c * FF_COLS, part * ff + (c + 1) * FF_COLS)

    def up_proj(c):
        for part in range(2):
            u_ref[c % n_slots, part] = _dot(hext, wup_ref[:, cols(c, part)])

    def conv(c, part):
        return _causal_taps(lambda a, b: u_ref[c % n_slots, part, a:b, :], cw_ref[:, cols(c, part)], tm,
                            cb_ref[:, cols(c, part)])

    acc = None
    for c in range(min(FFN_AHEAD, n_slabs)):
        up_proj(c)
    for c in range(n_slabs):
        if c + FFN_AHEAD < n_slabs:
            up_proj(c + FFN_AHEAD)
        d = _dot(_silu(conv(c, 0)) * conv(c, 1), wd_ref[c * FF_COLS:(c + 1) * FF_COLS, :])
        acc = d if acc is None else acc + d
    o_ref[0] = x_ref[0] + acc


def _conv_ffn(x, g, w_up, conv_w, conv_b, w_down, *, tm=512):
    B, S, D = x.shape
    ff = w_down.shape[0]
    assert ff % FF_COLS == 0 and S % tm == 0
    tile, halo = _halo_specs(tm, D)
    return pl.pallas_call(
        _ffn_kernel,
        grid=(B, S // tm),
        in_specs=[tile, halo, _resident((1, D)), _resident(w_up.shape), _resident(conv_w.shape),
                  _resident((1, 2 * ff)), _resident(w_down.shape)],
        out_specs=tile,
        out_shape=jax.ShapeDtypeStruct(x.shape, F32),
        scratch_shapes=[pltpu.VMEM((FFN_AHEAD + 1, 2, HALO + tm, FF_COLS), F32)],
        compiler_params=_params("parallel", "parallel"),
        name="conv_ffn",
    )(x, x, g[None, :], w_up.astype(BF16), conv_w, conv_b[None, :], w_down.astype(BF16))


def _chunk_cumsum(g):
    row = lax.broadcasted_iota(jnp.int32, g.shape, 0) & (GDN_CHUNK - 1)
    step = 1
    while step < GDN_CHUNK:
        g = g + jnp.where(row >= step, pltpu.roll(g, step, 0), 0.0)
        step *= 2
    return g


def _gdn_in_kernel(x_ref, xh_ref, g_ref, wqkv_ref, cw_ref, wz_ref, wab_ref, alog_ref, dtb_ref,
                   q_ref, k_ref, v_ref, z_ref, gcol_ref, grow_ref, u_ref):
    h, hext = _normed_with_halo(x_ref, xh_ref, g_ref)
    tm = x_ref.shape[1]
    qk_cols = GDN_HEADS * GDN_DK
    outs = (q_ref, k_ref, v_ref)
    for c in range(wqkv_ref.shape[1] // MXU_COLS):
        cols = slice(c * MXU_COLS, (c + 1) * MXU_COLS)
        u_ref[c % 2] = _dot(hext, wqkv_ref[:, cols])
        y = _silu(_causal_taps(lambda a, b: u_ref[c % 2, a:b, :], cw_ref[:, cols], tm))
        for half in range(2):
            col = c * MXU_COLS + half * LANES
            yy = y[:, half * LANES:(half + 1) * LANES]
            which, off = divmod(col, qk_cols)
            if which < 2:
                yy = yy * lax.rsqrt(jnp.sum(yy * yy, axis=-1, keepdims=True) + EPS)
            if which == 0:
                yy = yy * (GDN_DK ** -0.5)
            outs[which][0, :, off:off + LANES] = yy.astype(outs[which].dtype)
    for c in range(wz_ref.shape[1] // MXU_COLS):
        cols = slice(c * MXU_COLS, (c + 1) * MXU_COLS)
        z_ref[0, :, cols] = _dot(h, wz_ref[:, cols]).astype(z_ref.dtype)
    ab = _dot(h, wab_ref[...])
    t = ab + dtb_ref[...]
    softplus = jnp.maximum(t, 0.0) + jnp.log1p(jnp.exp(-jnp.abs(t)))
    cum = _chunk_cumsum(-jnp.exp(alog_ref[...]) * softplus)
    lane = lax.broadcasted_iota(jnp.int32, ab.shape, 1)
    gcol_ref[0] = jnp.where(lane < GDN_HEADS, cum, _sigmoid(ab))
    grow_ref[0] = cum.T[:GDN_HEADS, :]


def _gdn_in(x, g, w_in, conv_w, a_log, dt_bias, *, tm=512, act_dtype=BF16):
    B, S, D = x.shape
    H = GDN_HEADS
    qk = H * GDN_DK
    nqkv = conv_w.shape[1]
    nz = nqkv - 2 * qk
    wqkv = w_in[:, :nqkv].astype(BF16)
    wz = w_in[:, nqkv:nqkv + nz].astype(BF16)
    wab = jnp.pad(w_in[:, nqkv + nz:], ((0, 0), (0, LANES - 2 * H))).astype(BF16)
    alog = jnp.pad(a_log, (0, LANES - H))[None, :]
    dtb = jnp.pad(dt_bias, (0, LANES - H))[None, :]
    tile, halo = _halo_specs(tm, D)
    act = lambda n: jax.ShapeDtypeStruct((B, S, n), act_dtype)
    act_spec = lambda n: pl.BlockSpec((1, tm, n), lambda b, s: (b, s, 0))
    return pl.pallas_call(
        _gdn_in_kernel,
        grid=(B, S // tm),
        in_specs=[tile, halo, _resident((1, D)), _resident(wqkv.shape), _resident(conv_w.shape),
                  _resident(wz.shape), _resident(wab.shape), _resident(alog.shape), _resident(dtb.shape)],
        out_specs=[act_spec(qk), act_spec(qk), act_spec(nz), act_spec(nz), act_spec(LANES),
                   pl.BlockSpec((1, H, tm), lambda b, s: (b, 0, s))],
        out_shape=[act(qk), act(qk), act(nz), act(nz),
                   jax.ShapeDtypeStruct((B, S, LANES), F32), jax.ShapeDtypeStruct((B, H, S), F32)],
        scratch_shapes=[pltpu.VMEM((2, HALO + tm, MXU_COLS), F32)],
        compiler_params=_params("parallel", "parallel"),
        name="gdn_in",
    )(x, x, g[None, :], wqkv, conv_w, wz, wab, alog, dtb)


def _each(f, *lists):
    return [f(*args) for args in zip(*lists)]


def _same_block(ri, ci, size):
    shift = size.bit_length() - 1
    return (ri >> shift) == (ci >> shift)


def _unit_lower_inverses(a_s, ri, ci):
    eye = jnp.where(ri == ci, 1.0, 0.0)
    done = _each(lambda a: jnp.where(_same_block(ri, ci, SOLVE_BLOCK), a, 0.0), a_s)
    x = _each(lambda d: eye - d, done)
    power, order = done, 2
    while order < SOLVE_BLOCK:
        power = _each(_dot, power, power)
        x = _each(lambda x_, p: x_ + _dot(x_, p), x, power)
        order *= 2
    size = SOLVE_BLOCK
    while size < GDN_CHUNK:
        size *= 2
        part = a_s if size == GDN_CHUNK else _each(lambda a: jnp.where(_same_block(ri, ci, size), a, 0.0), a_s)
        t = _each(lambda p, d, x_: _dot(p - d, x_), part, done, x)
        x = _each(lambda x_, t_: x_ - _dot(x_, t_), x, t)
        done = part
    return x


def _gdn_chunk_kernel(q_ref, k_ref, v_ref, z_ref, gcol_ref, grow_ref, x_ref, onorm_ref, wout_ref,
                      o_ref, state_ref):
    R, C = GDN_ROWS, GDN_CHUNK
    ri = lax.broadcasted_iota(jnp.int32, (R, R), 0)
    ci = lax.broadcasted_iota(jnp.int32, (R, R), 1)
    same = _same_block(ri, ci, C)
    causal = same & (ri >= ci)
    strict = same & (ri > ci)
    gcol = gcol_ref[0]

    @pl.when(pl.program_id(1) == 0)
    def _():
        state_ref[...] = jnp.zeros(state_ref.shape, F32)

    heads = list(range(GDN_HEADS))
    head_lanes = [slice(h * LANES, (h + 1) * LANES) for h in heads]
    q = [q_ref[0, :, ln].astype(F32) for ln in head_lanes]
    k = [k_ref[0, :, ln].astype(F32) for ln in head_lanes]
    v = [v_ref[0, :, ln].astype(F32) for ln in head_lanes]
    gc = [gcol[:, h:h + 1] for h in heads]
    beta = [gcol[:, GDN_HEADS + h:GDN_HEADS + h + 1] for h in heads]
    decay = [jnp.exp(jnp.where(causal, g - grow_ref[0, h:h + 1, :], -jnp.inf)) for g, h in zip(gc, heads)]
    kb = [k_.astype(BF16) for k_ in k]
    kk = _each(_dot_nt, kb, kb)
    a = _each(lambda kk_, d, b: jnp.where(strict, kk_ * d * b, 0.0), kk, decay, beta)
    tinv = _unit_lower_inverses(a, ri, ci)
    attn = _each(lambda q_, kb_, d: _dot_nt(q_, kb_) * d, q, kb, decay)
    egc = [jnp.exp(g) for g in gc]
    g_last = [jnp.concatenate([jnp.broadcast_to(g[c * C + C - 1:(c + 1) * C, :], (C, 1)) for c in range(R // C)],
                              axis=0) for g in gc]
    sol = _each(lambda t, v_, k_, b, e: _dot(t, jnp.concatenate([v_ * b, k_ * (b * e)], axis=1)),
                tinv, v, k, beta, egc)
    u = [s_[:, :LANES] for s_ in sol]
    w = [s_[:, LANES:] for s_ in sol]
    q_dec = _each(lambda q_, e: q_ * e, q, egc)
    k_dec = _each(lambda k_, gl, g: k_ * jnp.exp(gl - g), k, g_last, gc)

    state = [state_ref[h] for h in heads]
    corrected = [[] for _ in heads]
    from_state = [[] for _ in heads]
    for c in range(R // C):
        rows = slice(c * C, (c + 1) * C)
        r = _each(lambda w_, qd, st: _dot(jnp.concatenate([w_[rows], qd[rows]], axis=0), st), w, q_dec, state)
        uc = _each(lambda u_, r_: u_[rows] - r_[:C], u, r)
        for j in heads:
            corrected[j].append(uc[j])
            from_state[j].append(r[j][C:])
        state = _each(lambda st, gl, kd, uc_: st * jnp.exp(gl[c * C:c * C + 1, :]) + _dot_tn(kd[rows], uc_),
                      state, g_last, k_dec, uc)
    for h in heads:
        state_ref[h] = state[h]
    o = _each(lambda fs, at, co: jnp.concatenate(fs, axis=0) + _dot(at, jnp.concatenate(co, axis=0)),
              from_state, attn, corrected)
    o = _each(lambda o_, ln: _rms(o_, onorm_ref[...]) * _silu(z_ref[0, :, ln].astype(F32)), o, head_lanes)
    o_ref[0] = x_ref[0] + _dot(jnp.concatenate([o_.astype(BF16) for o_ in o], axis=1), wout_ref[...])


def _gdn_chunk(x, q, k, v, z, gcol, grow, out_norm, w_out):
    B, S, D = x.shape
    H, R = GDN_HEADS, GDN_ROWS
    dv = w_out.shape[0] // H
    row_block = lambda n: pl.BlockSpec((1, R, n), lambda b, s: (b, s, 0))
    return pl.pallas_call(
        _gdn_chunk_kernel,
        grid=(B, S // R),
        in_specs=[row_block(q.shape[2]), row_block(k.shape[2]), row_block(v.shape[2]), row_block(z.shape[2]),
                  row_block(LANES), pl.BlockSpec((1, H, R), lambda b, s: (b, 0, s)),
                  row_block(D), _resident((1, dv)), _resident(w_out.shape)],
        out_specs=row_block(D),
        out_shape=jax.ShapeDtypeStruct(x.shape, F32),
        scratch_shapes=[pltpu.VMEM((H, GDN_DK, dv), F32)],
        compiler_params=_params("parallel", "arbitrary"),
        name="gdn_chunk",
    )(q, k, v, z, gcol, grow, x, out_norm[None, :], w_out.astype(BF16))


def _proj_kernel(x_ref, g_ref, w_ref, gain_ref, o_ref, ot_ref=None, *, n_normed):
    h = _rms(x_ref[0], g_ref[...]).astype(BF16)
    width = o_ref.shape[2]
    for c in range(w_ref.shape[1] // MXU_COLS):
        y = _dot(h, w_ref[:, c * MXU_COLS:(c + 1) * MXU_COLS])
        for half in range(2):
            col = c * MXU_COLS + half * LANES
            yy = y[:, half * LANES:(half + 1) * LANES]
            if col < n_normed:
                lane = lax.broadcasted_iota(jnp.int32, yy.shape, 1)
                low = lane < DIFF_DH
                sq = yy * yy
                ms = jnp.where(low,
                               jnp.sum(jnp.where(low, sq, 0.0), axis=-1, keepdims=True),
                               jnp.sum(jnp.where(low, 0.0, sq), axis=-1, keepdims=True)) * (1.0 / DIFF_DH)
                yy = yy * lax.rsqrt(ms + EPS) * gain_ref[:, col:col + LANES]
            if col < width:
                o_ref[0, :, col:col + LANES] = yy.astype(o_ref.dtype)
            else:
                head = (col - width) // LANES
                ot_ref[0, head, 0, :LANES, :] = yy.T.astype(ot_ref.dtype)
                ot_ref[0, head, 0, LANES:, :] = jnp.ones((ONES_ROWS, yy.shape[0]), ot_ref.dtype)


def _proj(x, g, w, gain, *, tm):
    B, S, D = x.shape
    width = gain.shape[0]
    heads_t = (w.shape[1] - width) // LANES
    tile = pl.BlockSpec((1, tm, D), lambda b, s: (b, s, 0))
    out_specs = [pl.BlockSpec((1, tm, width), lambda b, s: (b, s, 0))]
    out_shape = [jax.ShapeDtypeStruct((B, S, width), BF16)]
    if heads_t:
        rows = LANES + ONES_ROWS
        out_specs.append(pl.BlockSpec((1, heads_t, 1, rows, tm), lambda b, s: (b, 0, s, 0, 0)))
        out_shape.append(jax.ShapeDtypeStruct((B, heads_t, S // tm, rows, tm), BF16))
    return pl.pallas_call(
        functools.partial(_proj_kernel, n_normed=width),
        grid=(B, S // tm),
        in_specs=[tile, _resident((1, D)), _resident(w.shape), _resident((1, width))],
        out_specs=out_specs,
        out_shape=out_shape,
        compiler_params=_params("parallel", "parallel"),
        name="norm_proj",
    )(x, g[None, :], w.astype(BF16), gain[None, :])


def _attn_kernel(lam_ref, q_ref, k_ref, vt_ref, sub_ref, o_ref, m_ref, acc_ref, sa_ref, sb_ref, *, lam_init):
    i = pl.program_id(2)
    tq, dv = o_ref.shape[1:]
    tk = vt_ref.shape[4]
    n = tq // tk
    assert n % 2 == 0
    q = q_ref[0]
    lane = lax.broadcasted_iota(jnp.int32, q.shape, 1)
    qs = (jnp.where(lane < DIFF_DH, q, jnp.zeros_like(q)), jnp.where(lane < DIFF_DH, jnp.zeros_like(q), q))
    m_ref[...] = jnp.full(m_ref.shape, -jnp.inf, F32)
    acc_ref[...] = jnp.zeros(acc_ref.shape, F32)

    def scores(j, dst_ref, q0=0):
        kb = k_ref[0, pl.ds(pl.multiple_of(j * tk, tk), tk), :]
        for c in range(2):
            dst_ref[c, :, q0:] = _dot_nt(kb, qs[c][q0:])

    def consume(src_ref, j, q0=0, q1=tq, masked=False):
        vt = vt_ref[0, 0, j]
        for c in range(2):
            sc = src_ref[c, :, q0:q1]
            if masked:
                key = lax.broadcasted_iota(jnp.int32, sc.shape, 0)
                qry = lax.broadcasted_iota(jnp.int32, sc.shape, 1)
                sc = jnp.where(key <= qry, sc, -jnp.inf)
            m_prev = m_ref[c, :, q0:q1]
            m_new = jnp.maximum(m_prev, jnp.max(sc, axis=0, keepdims=True))
            p = jnp.exp2(sc - m_new)
            acc_ref[c, :, q0:q1] = jnp.exp2(m_prev - m_new) * acc_ref[c, :, q0:q1] + _dot(vt, p)
            m_ref[c, :, q0:q1] = m_new

    scores(0, sa_ref)

    def two_blocks(pair, carry):
        j = 2 * pair
        scores(j + 1, sb_ref)
        consume(sa_ref, j)
        scores(j + 2, sa_ref)
        consume(sb_ref, j + 1)
        return carry

    lax.fori_loop(0, i * (n // 2), two_blocks, 0)
    bufs = (sa_ref, sb_ref)
    for d in range(n):
        if d + 1 < n:
            scores(n * i + d + 1, bufs[(d + 1) % 2], (d + 1) * tk)
        consume(bufs[d % 2], n * i + d, d * tk, (d + 1) * tk, masked=True)
        if d + 1 < n:
            consume(bufs[d % 2], n * i + d, (d + 1) * tk, tq)

    lam = lam_ref[...]
    lam_full = (jnp.exp(jnp.sum(lam[0:1] * lam[1:2], axis=-1, keepdims=True))
                - jnp.exp(jnp.sum(lam[2:3] * lam[3:4], axis=-1, keepdims=True)) + lam_init)
    o = (acc_ref[0, :dv] / acc_ref[0, dv:dv + 1] - lam_full * (acc_ref[1, :dv] / acc_ref[1, dv:dv + 1]))
    ms = jnp.mean(o * o, axis=0, keepdims=True)
    o = o * lax.rsqrt(ms + EPS) * (sub_ref[...] * (1.0 - lam_init))
    o_ref[0] = o.T.astype(o_ref.dtype)


def _diff_attention(q, k, vt, lam, subln, lam_init):
    B, S, _ = q.shape
    H = DIFF_HEADS
    nk, rows, tk = vt.shape[2:]
    tq = ATTN_QUERY_BLOCKS * tk
    dv = subln.shape[0]
    k_spec = pl.BlockSpec((1, S, LANES), lambda b, h, i: (b, 0, h))
    vt_spec = pl.BlockSpec((1, 1, nk, rows, tk), lambda b, h, i: (b, h, 0, 0, 0))
    q_spec = pl.BlockSpec((1, tq, LANES), lambda b, h, i: (b, i, h))
    return pl.pallas_call(
        functools.partial(_attn_kernel, lam_init=lam_init),
        grid=(B, H, S // tq),
        in_specs=[_resident(lam.shape), q_spec, k_spec, vt_spec, _resident((dv, 1))],
        out_specs=q_spec,
        out_shape=jax.ShapeDtypeStruct(q.shape, BF16),
        scratch_shapes=[pltpu.VMEM((2, 1, tq), F32), pltpu.VMEM((2, rows, tq), F32),
                        pltpu.VMEM((2, tk, tq), F32), pltpu.VMEM((2, tk, tq), F32)],
        compiler_params=_params("parallel", "parallel", "parallel"),
        name="diff_attention",
    )(lam, q, k, vt, subln[:, None])


def _out_proj_kernel(x_ref, o_ref, w_ref, y_ref):
    y_ref[0] = x_ref[0] + _dot(o_ref[0], w_ref[...])


def _out_proj(x, o, w, *, tm=1024):
    B, S, D = x.shape
    tile = lambda n: pl.BlockSpec((1, tm, n), lambda b, s: (b, s, 0))
    return pl.pallas_call(
        _out_proj_kernel,
        grid=(B, S // tm),
        in_specs=[tile(D), tile(o.shape[2]), _resident(w.shape)],
        out_specs=tile(D),
        out_shape=jax.ShapeDtypeStruct(x.shape, F32),
        compiler_params=_params("parallel", "parallel"),
        name="out_proj",
    )(x, o, w.astype(BF16))


def kernel(x, a_norm, a_w_in, a_conv_w, a_log, a_dt_bias, a_out_norm, a_w_out, kv_norm, kv_w, k_norm,
           b_norm, b_w_q, b_q_norm, b_lambda, b_subln, b_w_out, f_norm, f_w_up, f_conv_w, f_conv_b,
           f_w_down):
    n_a = a_norm.shape[0]
    depth = f_norm.shape[0]
    groups = DIFF_HEADS * 2
    a_w_in, a_w_out, kv_w, b_w_q, b_w_out, f_w_up, f_w_down = (
        w.astype(BF16) for w in (a_w_in, a_w_out, kv_w, b_w_q, b_w_out, f_w_up, f_w_down))
    k_sh = vt_sh = None
    for i in range(depth):
        if i < n_a:
            q, k, v, z, gcol, grow = _gdn_in(x, a_norm[i], a_w_in[i], a_conv_w[i], a_log[i], a_dt_bias[i])
            x = _gdn_chunk(x, q, k, v, z, gcol, grow, a_out_norm[i], a_w_out[i])
        else:
            if i == n_a:
                k_sh, vt_sh = _proj(x, kv_norm, kv_w, jnp.tile(k_norm, groups), tm=ATTN_BLOCK)
            j = i - n_a
            lam_init = 0.8 - 0.6 * math.exp(-0.3 * i)
            q_gain = jnp.tile(b_q_norm[j], groups) * (DIFF_DH ** -0.5 * LOG2E)
            (q,) = _proj(x, b_norm[j], b_w_q[j], q_gain, tm=2 * ATTN_BLOCK)
            o = _diff_attention(q, k_sh, vt_sh, b_lambda[j], b_subln[j], lam_init)
            x = _out_proj(x, o, b_w_out[j])
        x = _conv_ffn(x, f_norm[i], f_w_up[i], f_conv_w[i], f_conv_b[i], f_w_down[i])
    return x
```

```python
import functools
import math

import jax
import jax.numpy as jnp
from jax import lax
from jax.experimental import pallas as pl
from jax.experimental.pallas import tpu as pltpu

F32 = jnp.float32
BF16 = jnp.bfloat16

EPS = 1e-6
LANES = 128
MXU_COLS = 256
HALO = 16
GDN_HEADS = 8
GDN_DK = 128
GDN_CHUNK = 64
GDN_ROWS = 256
DIFF_HEADS = 8
DIFF_DH = 64
ATTN_BLOCK = 512
ATTN_QUERY_BLOCKS = 4
LOG2E = math.log2(math.e)
ONES_ROWS = 16
FF_COLS = MXU_COLS
FFN_AHEAD = 2
FFN_DOWN_GROUP = 4
SOLVE_BLOCK = 16
VMEM_LIMIT = 56 * 1024 * 1024


def _params(*sem):
    return pltpu.CompilerParams(dimension_semantics=sem, vmem_limit_bytes=VMEM_LIMIT)


def _resident(shape):
    zeros = (0,) * len(shape)
    return pl.BlockSpec(shape, lambda *_: zeros, pipeline_mode=pl.Buffered(1))


def _rms(x, g):
    ms = jnp.mean(x * x, axis=-1, keepdims=True)
    return x * lax.rsqrt(ms + EPS) * g


def _sigmoid(x):
    return 1.0 / (1.0 + jnp.exp(-x))


def _silu(x):
    half = 0.5 * x
    return half * (1.0 + jnp.tanh(half))


def _dot(a, b):
    return jnp.dot(a.astype(BF16), b.astype(BF16), preferred_element_type=F32)


def _dot_nt(a, b):
    return lax.dot_general(a.astype(BF16), b.astype(BF16), (((1,), (1,)), ((), ())),
                           preferred_element_type=F32)


def _dot_tn(a, b):
    return lax.dot_general(a.astype(BF16), b.astype(BF16), (((0,), (0,)), ((), ())),
                           preferred_element_type=F32)


def _normed_with_halo(x_ref, xh_ref, g_ref):
    g = g_ref[...]
    h = _rms(x_ref[0], g).astype(BF16)
    keep = (pl.program_id(1) > 0).astype(F32)
    hh = (_rms(xh_ref[0], g) * keep).astype(BF16)
    return h, jnp.concatenate([hh, h], axis=0)


def _causal_taps(rows, cw, tm, bias=None):
    taps = cw.shape[0]
    y = bias
    for d in range(taps):
        term = rows(HALO - d, HALO - d + tm) * cw[taps - 1 - d:taps - d]
        y = term if y is None else y + term
    return y


def _halo_specs(tm, d):
    tile = pl.BlockSpec((1, tm, d), lambda b, s: (b, s, 0))
    halo = pl.BlockSpec((1, HALO, d), lambda b, s: (b, jnp.maximum(s * (tm // HALO) - 1, 0), 0))
    return tile, halo


def _ffn_kernel(x_ref, g_ref, wup_ref, cw_ref, cb_ref, wd_ref, o_ref, u_ref, hist_ref, act_ref):
    h = _rms(x_ref[0], g_ref[...]).astype(BF16)
    tm = x_ref.shape[1]
    n_slots = u_ref.shape[0]
    ff = wd_ref.shape[0]
    n_slabs = ff // FF_COLS

    @pl.when(pl.program_id(1) == 0)
    def _():
        hist_ref[...] = jnp.zeros(hist_ref.shape, F32)

    def cols(c, part):
        return slice(part * ff + c * FF_COLS, part * ff + (c + 1) * FF_COLS)

    def up_proj(c):
        for part in range(2):
            u_ref[c % n_slots, part, :HALO] = hist_ref[c, part]
            u_ref[c % n_slots, part, HALO:] = _dot(h, wup_ref[:, cols(c, part)])

    def conv(c, part):
        y = _causal_taps(lambda a, b: u_ref[c % n_slots, part, a:b, :], cw_ref[:, cols(c, part)], tm,
                         cb_ref[:, cols(c, part)])
        hist_ref[c, part] = u_ref[c % n_slots, part, tm:, :]
        return y

    acc = x_ref[0]
    for c in range(min(FFN_AHEAD, n_slabs)):
        up_proj(c)
    for c in range(n_slabs):
        if c + FFN_AHEAD < n_slabs:
            up_proj(c + FFN_AHEAD)
        group, first = divmod(c, FFN_DOWN_GROUP)
        first *= FF_COLS
        act_ref[group % 2, :, first:first + FF_COLS] = (_silu(conv(c, 0)) * conv(c, 1)).astype(BF16)
        if (c + 1) % FFN_DOWN_GROUP == 0 or c + 1 == n_slabs:
            width = first + FF_COLS
            start = group * FFN_DOWN_GROUP * FF_COLS
            acc = acc + _dot(act_ref[group % 2, :, :width], wd_ref[start:start + width, :])
    o_ref[0] = acc


def _conv_ffn(x, g, w_up, conv_w, conv_b, w_down, *, tm=512):
    B, S, D = x.shape
    ff = w_down.shape[0]
    assert ff % FF_COLS == 0 and S % tm == 0
    tile = pl.BlockSpec((1, tm, D), lambda b, s: (b, s, 0))
    return pl.pallas_call(
        _ffn_kernel,
        grid=(B, S // tm),
        in_specs=[tile, _resident((1, D)), _resident(w_up.shape), _resident(conv_w.shape),
                  _resident((1, 2 * ff)), _resident(w_down.shape)],
        out_specs=tile,
        out_shape=jax.ShapeDtypeStruct(x.shape, F32),
        scratch_shapes=[pltpu.VMEM((FFN_AHEAD + 1, 2, HALO + tm, FF_COLS), F32),
                        pltpu.VMEM((ff // FF_COLS, 2, HALO, FF_COLS), F32),
                        pltpu.VMEM((2, tm, FFN_DOWN_GROUP * FF_COLS), BF16)],
        compiler_params=_params("parallel", "arbitrary"),
        name="conv_ffn",
    )(x, g[None, :], w_up.astype(BF16), conv_w, conv_b[None, :], w_down.astype(BF16))


def _chunk_cumsum(g):
    row = lax.broadcasted_iota(jnp.int32, g.shape, 0) & (GDN_CHUNK - 1)
    step = 1
    while step < GDN_CHUNK:
        g = g + jnp.where(row >= step, pltpu.roll(g, step, 0), 0.0)
        step *= 2
    return g


def _gdn_in_kernel(x_ref, xh_ref, g_ref, wqkv_ref, cw_ref, wz_ref, wab_ref, alog_ref, dtb_ref,
                   q_ref, k_ref, v_ref, z_ref, gcol_ref, grow_ref, u_ref):
    h, hext = _normed_with_halo(x_ref, xh_ref, g_ref)
    tm = x_ref.shape[1]
    qk_cols = GDN_HEADS * GDN_DK
    outs = (q_ref, k_ref, v_ref)
    n_chunks = wqkv_ref.shape[1] // MXU_COLS
    n_z = wz_ref.shape[1] // MXU_COLS
    assert n_z <= n_chunks

    def tile_cols(c):
        return slice(c * MXU_COLS, (c + 1) * MXU_COLS)

    def project(c):
        u_ref[c % 2] = _dot(hext, wqkv_ref[:, tile_cols(c)])

    project(0)
    for c in range(n_chunks):
        cols = tile_cols(c)
        if c + 1 < n_chunks:
            project(c + 1)
        if (c + 1) * n_z // n_chunks > c * n_z // n_chunks:
            zc = tile_cols(c * n_z // n_chunks)
            z_ref[0, :, zc] = _dot(h, wz_ref[:, zc]).astype(z_ref.dtype)
        y = _silu(_causal_taps(lambda a, b: u_ref[c % 2, a:b, :], cw_ref[:, cols], tm))
        for half in range(2):
            col = c * MXU_COLS + half * LANES
            yy = y[:, half * LANES:(half + 1) * LANES]
            which, off = divmod(col, qk_cols)
            if which < 2:
                yy = yy * lax.rsqrt(jnp.sum(yy * yy, axis=-1, keepdims=True) + EPS)
            if which == 0:
                yy = yy * (GDN_DK ** -0.5)
            outs[which][0, :, off:off + LANES] = yy.astype(outs[which].dtype)
    ab = _dot(h, wab_ref[...])
    t = ab + dtb_ref[...]
    softplus = jnp.maximum(t, 0.0) + jnp.log1p(jnp.exp(-jnp.abs(t)))
    cum = _chunk_cumsum(-jnp.exp(alog_ref[...]) * softplus)
    lane = lax.broadcasted_iota(jnp.int32, ab.shape, 1)
    gcol_ref[0] = jnp.where(lane < GDN_HEADS, cum, _sigmoid(ab))
    grow_ref[0] = cum.T[:GDN_HEADS, :]


def _gdn_in(x, g, w_in, conv_w, a_log, dt_bias, *, tm=512, act_dtype=BF16):
    B, S, D = x.shape
    H = GDN_HEADS
    qk = H * GDN_DK
    nqkv = conv_w.shape[1]
    nz = nqkv - 2 * qk
    wqkv = w_in[:, :nqkv].astype(BF16)
    wz = w_in[:, nqkv:nqkv + nz].astype(BF16)
    wab = jnp.pad(w_in[:, nqkv + nz:], ((0, 0), (0, LANES - 2 * H))).astype(BF16)
    alog = jnp.pad(a_log, (0, LANES - H))[None, :]
    dtb = jnp.pad(dt_bias, (0, LANES - H))[None, :]
    tile, halo = _halo_specs(tm, D)
    act = lambda n: jax.ShapeDtypeStruct((B, S, n), act_dtype)
    act_spec = lambda n: pl.BlockSpec((1, tm, n), lambda b, s: (b, s, 0))
    return pl.pallas_call(
        _gdn_in_kernel,
        grid=(B, S // tm),
        in_specs=[tile, halo, _resident((1, D)), _resident(wqkv.shape), _resident(conv_w.shape),
                  _resident(wz.shape), _resident(wab.shape), _resident(alog.shape), _resident(dtb.shape)],
        out_specs=[act_spec(qk), act_spec(qk), act_spec(nz), act_spec(nz), act_spec(LANES),
                   pl.BlockSpec((1, H, tm), lambda b, s: (b, 0, s))],
        out_shape=[act(qk), act(qk), act(nz), act(nz),
                   jax.ShapeDtypeStruct((B, S, LANES), F32), jax.ShapeDtypeStruct((B, H, S), F32)],
        scratch_shapes=[pltpu.VMEM((2, HALO + tm, MXU_COLS), F32)],
        compiler_params=_params("parallel", "parallel"),
        name="gdn_in",
    )(x, x, g[None, :], wqkv, conv_w, wz, wab, alog, dtb)


def _each(f, *lists):
    return [f(*args) for args in zip(*lists)]


def _same_block(ri, ci, size):
    shift = size.bit_length() - 1
    return (ri >> shift) == (ci >> shift)


def _unit_lower_inverses(a_s, ri, ci):
    eye = jnp.where(ri == ci, 1.0, 0.0)
    done = _each(lambda a: jnp.where(_same_block(ri, ci, SOLVE_BLOCK), a, 0.0), a_s)
    x = _each(lambda d: eye - d, done)
    power, order = done, 2
    while order < SOLVE_BLOCK:
        power = _each(_dot, power, power)
        x = _each(lambda x_, p: x_ + _dot(x_, p), x, power)
        order *= 2
    size = SOLVE_BLOCK
    while size < GDN_CHUNK:
        size *= 2
        part = a_s if size == GDN_CHUNK else _each(lambda a: jnp.where(_same_block(ri, ci, size), a, 0.0), a_s)
        t = _each(lambda p, d, x_: _dot(p - d, x_), part, done, x)
        x = _each(lambda x_, t_: x_ - _dot(x_, t_), x, t)
        done = part
    return x


def _gdn_chunk_kernel(q_ref, k_ref, v_ref, z_ref, gcol_ref, grow_ref, x_ref, onorm_ref, wout_ref,
                      o_ref, state_ref):
    R, C = GDN_ROWS, GDN_CHUNK
    ri = lax.broadcasted_iota(jnp.int32, (R, R), 0)
    ci = lax.broadcasted_iota(jnp.int32, (R, R), 1)
    same = _same_block(ri, ci, C)
    causal = same & (ri >= ci)
    strict = same & (ri > ci)
    gcol = gcol_ref[0]

    @pl.when(pl.program_id(1) == 0)
    def _():
        state_ref[...] = jnp.zeros(state_ref.shape, F32)

    heads = list(range(GDN_HEADS))
    head_lanes = [slice(h * LANES, (h + 1) * LANES) for h in heads]
    q = [q_ref[0, :, ln].astype(F32) for ln in head_lanes]
    k = [k_ref[0, :, ln].astype(F32) for ln in head_lanes]
    v = [v_ref[0, :, ln].astype(F32) for ln in head_lanes]
    gc = [gcol[:, h:h + 1] for h in heads]
    beta = [gcol[:, GDN_HEADS + h:GDN_HEADS + h + 1] for h in heads]
    decay = [jnp.exp(jnp.where(causal, g - grow_ref[0, h:h + 1, :], -jnp.inf)) for g, h in zip(gc, heads)]
    kb = [k_.astype(BF16) for k_ in k]
    kk = _each(_dot_nt, kb, kb)
    a = _each(lambda kk_, d, b: jnp.where(strict, kk_ * d * b, 0.0), kk, decay, beta)
    tinv = _unit_lower_inverses(a, ri, ci)
    attn = _each(lambda q_, kb_, d: _dot_nt(q_, kb_) * d, q, kb, decay)
    egc = [jnp.exp(g) for g in gc]
    g_last = [jnp.concatenate([jnp.broadcast_to(g[c * C + C - 1:(c + 1) * C, :], (C, 1)) for c in range(R // C)],
                              axis=0) for g in gc]
    sol = _each(lambda t, v_, k_, b, e: _dot(t, jnp.concatenate([v_ * b, k_ * (b * e)], axis=1)),
                tinv, v, k, beta, egc)
    u = [s_[:, :LANES] for s_ in sol]
    w = [s_[:, LANES:] for s_ in sol]
    q_dec = _each(lambda q_, e: q_ * e, q, egc)
    k_dec = _each(lambda k_, gl, g: k_ * jnp.exp(gl - g), k, g_last, gc)

    state = [state_ref[h] for h in heads]
    corrected = [[] for _ in heads]
    from_state = [[] for _ in heads]
    for c in range(R // C):
        rows = slice(c * C, (c + 1) * C)
        r = _each(lambda w_, qd, st: _dot(jnp.concatenate([w_[rows], qd[rows]], axis=0), st), w, q_dec, state)
        uc = _each(lambda u_, r_: u_[rows] - r_[:C], u, r)
        for j in heads:
            corrected[j].append(uc[j])
            from_state[j].append(r[j][C:])
        state = _each(lambda st, gl, kd, uc_: st * jnp.exp(gl[c * C:c * C + 1, :]) + _dot_tn(kd[rows], uc_),
                      state, g_last, k_dec, uc)
    for h in heads:
        state_ref[h] = state[h]
    o = _each(lambda fs, at, co: jnp.concatenate(fs, axis=0) + _dot(at, jnp.concatenate(co, axis=0)),
              from_state, attn, corrected)
    o = _each(lambda o_, ln: _rms(o_, onorm_ref[...]) * _silu(z_ref[0, :, ln].astype(F32)), o, head_lanes)
    o_ref[0] = x_ref[0] + _dot(jnp.concatenate([o_.astype(BF16) for o_ in o], axis=1), wout_ref[...])


def _gdn_chunk(x, q, k, v, z, gcol, grow, out_norm, w_out):
    B, S, D = x.shape
    H, R = GDN_HEADS, GDN_ROWS
    dv = w_out.shape[0] // H
    row_block = lambda n: pl.BlockSpec((1, R, n), lambda b, s: (b, s, 0))
    return pl.pallas_call(
        _gdn_chunk_kernel,
        grid=(B, S // R),
        in_specs=[row_block(q.shape[2]), row_block(k.shape[2]), row_block(v.shape[2]), row_block(z.shape[2]),
                  row_block(LANES), pl.BlockSpec((1, H, R), lambda b, s: (b, 0, s)),
                  row_block(D), _resident((1, dv)), _resident(w_out.shape)],
        out_specs=row_block(D),
        out_shape=jax.ShapeDtypeStruct(x.shape, F32),
        scratch_shapes=[pltpu.VMEM((H, GDN_DK, dv), F32)],
        compiler_params=_params("parallel", "arbitrary"),
        name="gdn_chunk",
    )(q, k, v, z, gcol, grow, x, out_norm[None, :], w_out.astype(BF16))


def _proj_kernel(x_ref, g_ref, w_ref, gain_ref, o_ref, ot_ref=None, *, n_normed):
    h = _rms(x_ref[0], g_ref[...]).astype(BF16)
    width = o_ref.shape[2]
    for c in range(w_ref.shape[1] // MXU_COLS):
        y = _dot(h, w_ref[:, c * MXU_COLS:(c + 1) * MXU_COLS])
        for half in range(2):
            col = c * MXU_COLS + half * LANES
            yy = y[:, half * LANES:(half + 1) * LANES]
            if col < n_normed:
                lane = lax.broadcasted_iota(jnp.int32, yy.shape, 1)
                low = lane < DIFF_DH
                sq = yy * yy
                ms = jnp.where(low,
                               jnp.sum(jnp.where(low, sq, 0.0), axis=-1, keepdims=True),
                               jnp.sum(jnp.where(low, 0.0, sq), axis=-1, keepdims=True)) * (1.0 / DIFF_DH)
                yy = yy * lax.rsqrt(ms + EPS) * gain_ref[:, col:col + LANES]
            if col < width:
                o_ref[0, :, col:col + LANES] = yy.astype(o_ref.dtype)
            else:
                head = (col - width) // LANES
                ot_ref[0, head, 0, :LANES, :] = yy.T.astype(ot_ref.dtype)
                ot_ref[0, head, 0, LANES:, :] = jnp.ones((ONES_ROWS, yy.shape[0]), ot_ref.dtype)


def _proj(x, g, w, gain, *, tm):
    B, S, D = x.shape
    width = gain.shape[0]
    heads_t = (w.shape[1] - width) // LANES
    tile = pl.BlockSpec((1, tm, D), lambda b, s: (b, s, 0))
    out_specs = [pl.BlockSpec((1, tm, width), lambda b, s: (b, s, 0))]
    out_shape = [jax.ShapeDtypeStruct((B, S, width), BF16)]
    if heads_t:
        rows = LANES + ONES_ROWS
        out_specs.append(pl.BlockSpec((1, heads_t, 1, rows, tm), lambda b, s: (b, 0, s, 0, 0)))
        out_shape.append(jax.ShapeDtypeStruct((B, heads_t, S // tm, rows, tm), BF16))
    return pl.pallas_call(
        functools.partial(_proj_kernel, n_normed=width),
        grid=(B, S // tm),
        in_specs=[tile, _resident((1, D)), _resident(w.shape), _resident((1, width))],
        out_specs=out_specs,
        out_shape=out_shape,
        compiler_params=_params("parallel", "parallel"),
        name="norm_proj",
    )(x, g[None, :], w.astype(BF16), gain[None, :])


def _attn_kernel(lam_ref, q_ref, k_ref, vt_ref, sub_ref, o_ref, m_ref, acc_ref, sa_ref, sb_ref, *, lam_init):
    i = pl.program_id(2)
    tq, dv = o_ref.shape[1:]
    tk = vt_ref.shape[4]
    n = tq // tk
    assert n % 2 == 0
    q = q_ref[0]
    lane = lax.broadcasted_iota(jnp.int32, q.shape, 1)
    qs = (jnp.where(lane < DIFF_DH, q, jnp.zeros_like(q)), jnp.where(lane < DIFF_DH, jnp.zeros_like(q), q))
    m_ref[...] = jnp.full(m_ref.shape, -jnp.inf, F32)
    acc_ref[...] = jnp.zeros(acc_ref.shape, F32)

    def scores(j, dst_ref, q0=0):
        kb = k_ref[0, pl.ds(pl.multiple_of(j * tk, tk), tk), :]
        for c in range(2):
            dst_ref[c, :, q0:] = _dot_nt(kb, qs[c][q0:])

    def consume(src_ref, j, q0=0, q1=tq, masked=False):
        vt = vt_ref[0, 0, j]
        for c in range(2):
            sc = src_ref[c, :, q0:q1]
            if masked:
                key = lax.broadcasted_iota(jnp.int32, sc.shape, 0)
                qry = lax.broadcasted_iota(jnp.int32, sc.shape, 1)
                sc = jnp.where(key <= qry, sc, -jnp.inf)
            m_prev = m_ref[c, :, q0:q1]
            m_new = jnp.maximum(m_prev, jnp.max(sc, axis=0, keepdims=True))
            p = jnp.exp2(sc - m_new)
            acc_ref[c, :, q0:q1] = jnp.exp2(m_prev - m_new) * acc_ref[c, :, q0:q1] + _dot(vt, p)
            m_ref[c, :, q0:q1] = m_new

    scores(0, sa_ref)

    def two_blocks(pair, carry):
        j = 2 * pair
        scores(j + 1, sb_ref)
        consume(sa_ref, j)
        scores(j + 2, sa_ref)
        consume(sb_ref, j + 1)
        return carry

    lax.fori_loop(0, i * (n // 2), two_blocks, 0)
    bufs = (sa_ref, sb_ref)
    for d in range(n):
        if d + 1 < n:
            scores(n * i + d + 1, bufs[(d + 1) % 2], (d + 1) * tk)
        consume(bufs[d % 2], n * i + d, d * tk, (d + 1) * tk, masked=True)
        if d + 1 < n:
            consume(bufs[d % 2], n * i + d, (d + 1) * tk, tq)

    lam = lam_ref[...]
    lam_full = (jnp.exp(jnp.sum(lam[0:1] * lam[1:2], axis=-1, keepdims=True))
                - jnp.exp(jnp.sum(lam[2:3] * lam[3:4], axis=-1, keepdims=True)) + lam_init)
    o = (acc_ref[0, :dv] / acc_ref[0, dv:dv + 1] - lam_full * (acc_ref[1, :dv] / acc_ref[1, dv:dv + 1]))
    ms = jnp.mean(o * o, axis=0, keepdims=True)
    o = o * lax.rsqrt(ms + EPS) * (sub_ref[...] * (1.0 - lam_init))
    o_ref[0] = o.T.astype(o_ref.dtype)


def _diff_attention(q, k, vt, lam, subln, lam_init):
    B, S, _ = q.shape
    H = DIFF_HEADS
    nk, rows, tk = vt.shape[2:]
    tq = ATTN_QUERY_BLOCKS * tk
    dv = subln.shape[0]
    k_spec = pl.BlockSpec((1, S, LANES), lambda b, h, i: (b, 0, h))
    vt_spec = pl.BlockSpec((1, 1, nk, rows, tk), lambda b, h, i: (b, h, 0, 0, 0))
    q_spec = pl.BlockSpec((1, tq, LANES), lambda b, h, i: (b, i, h))
    return pl.pallas_call(
        functools.partial(_attn_kernel, lam_init=lam_init),
        grid=(B, H, S // tq),
        in_specs=[_resident(lam.shape), q_spec, k_spec, vt_spec, _resident((dv, 1))],
        out_specs=q_spec,
        out_shape=jax.ShapeDtypeStruct(q.shape, BF16),
        scratch_shapes=[pltpu.VMEM((2, 1, tq), F32), pltpu.VMEM((2, rows, tq), F32),
                        pltpu.VMEM((2, tk, tq), F32), pltpu.VMEM((2, tk, tq), F32)],
        compiler_params=_params("parallel", "parallel", "parallel"),
        name="diff_attention",
    )(lam, q, k, vt, subln[:, None])


def _out_proj_kernel(x_ref, o_ref, w_ref, y_ref):
    y_ref[0] = x_ref[0] + _dot(o_ref[0], w_ref[...])


def _out_proj(x, o, w, *, tm=1024):
    B, S, D = x.shape
    tile = lambda n: pl.BlockSpec((1, tm, n), lambda b, s: (b, s, 0))
    return pl.pallas_call(
        _out_proj_kernel,
        grid=(B, S // tm),
        in_specs=[tile(D), tile(o.shape[2]), _resident(w.shape)],
        out_specs=tile(D),
        out_shape=jax.ShapeDtypeStruct(x.shape, F32),
        compiler_params=_params("parallel", "parallel"),
        name="out_proj",
    )(x, o, w.astype(BF16))


def kernel(x, a_norm, a_w_in, a_conv_w, a_log, a_dt_bias, a_out_norm, a_w_out, kv_norm, kv_w, k_norm,
           b_norm, b_w_q, b_q_norm, b_lambda, b_subln, b_w_out, f_norm, f_w_up, f_conv_w, f_conv_b,
           f_w_down):
    n_a = a_norm.shape[0]
    depth = f_norm.shape[0]
    groups = DIFF_HEADS * 2
    a_w_in, a_w_out, kv_w, b_w_q, b_w_out, f_w_up, f_w_down = (
        w.astype(BF16) for w in (a_w_in, a_w_out, kv_w, b_w_q, b_w_out, f_w_up, f_w_down))
    k_sh = vt_sh = None
    for i in range(depth):
        if i < n_a:
            q, k, v, z, gcol, grow = _gdn_in(x, a_norm[i], a_w_in[i], a_conv_w[i], a_log[i], a_dt_bias[i])
            x = _gdn_chunk(x, q, k, v, z, gcol, grow, a_out_norm[i], a_w_out[i])
        else:
            if i == n_a:
                k_sh, vt_sh = _proj(x, kv_norm, kv_w, jnp.tile(k_norm, groups), tm=ATTN_BLOCK)
            j = i - n_a
            lam_init = 0.8 - 0.6 * math.exp(-0.3 * i)
            q_gain = jnp.tile(b_q_norm[j], groups) * (DIFF_DH ** -0.5 * LOG2E)
            (q,) = _proj(x, b_norm[j], b_w_q[j], q_gain, tm=2 * ATTN_BLOCK)
            o = _diff_attention(q, k_sh, vt_sh, b_lambda[j], b_subln[j], lam_init)
            x = _out_proj(x, o, b_w_out[j])
        x = _conv_ffn(x, f_norm[i], f_w_up[i], f_conv_w[i], f_conv_b[i], f_w_down[i])
    return x
```

```python
import functools
import math

import jax
import jax.numpy as jnp
from jax import lax
from jax.experimental import pallas as pl
from jax.experimental.pallas import tpu as pltpu

F32 = jnp.float32
BF16 = jnp.bfloat16

EPS = 1e-6
LANES = 128
MXU_COLS = 256
HALO = 16
GDN_HEADS = 8
GDN_DK = 128
GDN_CHUNK = 64
GDN_ROWS = 256
DIFF_HEADS = 8
DIFF_DH = 64
ATTN_BLOCK = 512
ATTN_QUERY_BLOCKS = 4
LOG2E = math.log2(math.e)
ONES_ROWS = 16
FF_COLS = MXU_COLS
FFN_AHEAD = 3
FFN_DOWN_GROUP = 4
SOLVE_BLOCK = 16
VMEM_LIMIT = 56 * 1024 * 1024


def _params(*sem):
    return pltpu.CompilerParams(dimension_semantics=sem, vmem_limit_bytes=VMEM_LIMIT)


def _resident(shape):
    zeros = (0,) * len(shape)
    return pl.BlockSpec(shape, lambda *_: zeros, pipeline_mode=pl.Buffered(1))


def _rms(x, g):
    ms = jnp.mean(x * x, axis=-1, keepdims=True)
    return x * lax.rsqrt(ms + EPS) * g


def _sigmoid(x):
    return 1.0 / (1.0 + jnp.exp(-x))


def _silu(x):
    half = 0.5 * x
    return half * (1.0 + jnp.tanh(half))


def _dot(a, b):
    return jnp.dot(a.astype(BF16), b.astype(BF16), preferred_element_type=F32)


def _dot_nt(a, b):
    return lax.dot_general(a.astype(BF16), b.astype(BF16), (((1,), (1,)), ((), ())),
                           preferred_element_type=F32)


def _dot_tn(a, b):
    return lax.dot_general(a.astype(BF16), b.astype(BF16), (((0,), (0,)), ((), ())),
                           preferred_element_type=F32)


def _normed_with_halo(x_ref, xh_ref, g_ref):
    g = g_ref[...]
    h = _rms(x_ref[0], g).astype(BF16)
    keep = (pl.program_id(1) > 0).astype(F32)
    hh = (_rms(xh_ref[0], g) * keep).astype(BF16)
    return h, jnp.concatenate([hh, h], axis=0)


def _causal_taps(rows, cw, tm, bias=None):
    taps = cw.shape[0]
    y = bias
    for d in range(taps):
        term = rows(HALO - d, HALO - d + tm) * cw[taps - 1 - d:taps - d]
        y = term if y is None else y + term
    return y


def _halo_specs(tm, d):
    tile = pl.BlockSpec((1, tm, d), lambda b, s: (b, s, 0))
    halo = pl.BlockSpec((1, HALO, d), lambda b, s: (b, jnp.maximum(s * (tm // HALO) - 1, 0), 0))
    return tile, halo


def _ffn_kernel(x_ref, g_ref, wup_ref, cw_ref, cb_ref, wd_ref, o_ref, u_ref, hist_ref, act_ref):
    h = _rms(x_ref[0], g_ref[...]).astype(BF16)
    tm = x_ref.shape[1]
    n_slots = u_ref.shape[0]
    ff = wd_ref.shape[0]
    n_slabs = ff // FF_COLS

    @pl.when(pl.program_id(1) == 0)
    def _():
        hist_ref[...] = jnp.zeros(hist_ref.shape, F32)

    def cols(c, part):
        return slice(part * ff + c * FF_COLS, part * ff + (c + 1) * FF_COLS)

    def up_proj(c):
        for part in range(2):
            u_ref[c % n_slots, part, :HALO] = hist_ref[c, part]
            u_ref[c % n_slots, part, HALO:] = _dot(h, wup_ref[:, cols(c, part)])

    def conv(c, part):
        y = _causal_taps(lambda a, b: u_ref[c % n_slots, part, a:b, :], cw_ref[:, cols(c, part)], tm,
                         cb_ref[:, cols(c, part)])
        hist_ref[c, part] = u_ref[c % n_slots, part, tm:, :]
        return y

    acc = x_ref[0]
    for c in range(min(FFN_AHEAD, n_slabs)):
        up_proj(c)
    for c in range(n_slabs):
        if c + FFN_AHEAD < n_slabs:
            up_proj(c + FFN_AHEAD)
        group, first = divmod(c, FFN_DOWN_GROUP)
        first *= FF_COLS
        act_ref[group % 2, :, first:first + FF_COLS] = (_silu(conv(c, 0)) * conv(c, 1)).astype(BF16)
        if (c + 1) % FFN_DOWN_GROUP == 0 or c + 1 == n_slabs:
            width = first + FF_COLS
            start = group * FFN_DOWN_GROUP * FF_COLS
            acc = acc + _dot(act_ref[group % 2, :, :width], wd_ref[start:start + width, :])
    o_ref[0] = acc


def _conv_ffn(x, g, w_up, conv_w, conv_b, w_down, *, tm=512):
    B, S, D = x.shape
    ff = w_down.shape[0]
    assert ff % FF_COLS == 0 and S % tm == 0
    tile = pl.BlockSpec((1, tm, D), lambda b, s: (b, s, 0))
    return pl.pallas_call(
        _ffn_kernel,
        grid=(B, S // tm),
        in_specs=[tile, _resident((1, D)), _resident(w_up.shape), _resident(conv_w.shape),
                  _resident((1, 2 * ff)), _resident(w_down.shape)],
        out_specs=tile,
        out_shape=jax.ShapeDtypeStruct(x.shape, F32),
        scratch_shapes=[pltpu.VMEM((FFN_AHEAD + 1, 2, HALO + tm, FF_COLS), F32),
                        pltpu.VMEM((ff // FF_COLS, 2, HALO, FF_COLS), F32),
                        pltpu.VMEM((2, tm, FFN_DOWN_GROUP * FF_COLS), BF16)],
        compiler_params=_params("parallel", "arbitrary"),
        name="conv_ffn",
    )(x, g[None, :], w_up.astype(BF16), conv_w, conv_b[None, :], w_down.astype(BF16))


def _chunk_cumsum(g):
    row = lax.broadcasted_iota(jnp.int32, g.shape, 0) & (GDN_CHUNK - 1)
    step = 1
    while step < GDN_CHUNK:
        g = g + jnp.where(row >= step, pltpu.roll(g, step, 0), 0.0)
        step *= 2
    return g


def _gdn_in_kernel(x_ref, xh_ref, g_ref, wqkv_ref, cw_ref, wz_ref, wab_ref, alog_ref, dtb_ref,
                   q_ref, k_ref, v_ref, z_ref, gcol_ref, grow_ref, u_ref):
    h, hext = _normed_with_halo(x_ref, xh_ref, g_ref)
    tm = x_ref.shape[1]
    qk_cols = GDN_HEADS * GDN_DK
    outs = (q_ref, k_ref, v_ref)
    n_chunks = wqkv_ref.shape[1] // MXU_COLS
    n_z = wz_ref.shape[1] // MXU_COLS
    assert n_z <= n_chunks and qk_cols % MXU_COLS == 0

    def tile_cols(c):
        return slice(c * MXU_COLS, (c + 1) * MXU_COLS)

    def project(c):
        u_ref[c % 2] = _dot(hext, wqkv_ref[:, tile_cols(c)])

    project(0)
    for c in range(n_chunks):
        cols = tile_cols(c)
        if c + 1 < n_chunks:
            project(c + 1)
        if (c + 1) * n_z // n_chunks > c * n_z // n_chunks:
            zc = tile_cols(c * n_z // n_chunks)
            z_ref[0, :, zc] = _dot(h, wz_ref[:, zc]).astype(z_ref.dtype)
        y = _causal_taps(lambda a, b: u_ref[c % 2, a:b, :], cw_ref[:, cols], tm)
        which, off = divmod(c * MXU_COLS, qk_cols)
        outs[which][0, :, off:off + MXU_COLS] = y.astype(outs[which].dtype)
    ab = _dot(h, wab_ref[...])
    t = ab + dtb_ref[...]
    softplus = jnp.maximum(t, 0.0) + jnp.log1p(jnp.exp(-jnp.abs(t)))
    cum = _chunk_cumsum(-jnp.exp(alog_ref[...]) * softplus)
    lane = lax.broadcasted_iota(jnp.int32, ab.shape, 1)
    gcol_ref[0] = jnp.where(lane < GDN_HEADS, cum, _sigmoid(ab))
    grow_ref[0] = cum.T[:GDN_HEADS, :]


def _gdn_in(x, g, w_in, conv_w, a_log, dt_bias, *, tm=512, act_dtype=BF16):
    B, S, D = x.shape
    H = GDN_HEADS
    qk = H * GDN_DK
    nqkv = conv_w.shape[1]
    nz = nqkv - 2 * qk
    wqkv = w_in[:, :nqkv].astype(BF16)
    wz = w_in[:, nqkv:nqkv + nz].astype(BF16)
    wab = jnp.pad(w_in[:, nqkv + nz:], ((0, 0), (0, LANES - 2 * H))).astype(BF16)
    alog = jnp.pad(a_log, (0, LANES - H))[None, :]
    dtb = jnp.pad(dt_bias, (0, LANES - H))[None, :]
    tile, halo = _halo_specs(tm, D)
    act = lambda n: jax.ShapeDtypeStruct((B, S, n), act_dtype)
    act_spec = lambda n: pl.BlockSpec((1, tm, n), lambda b, s: (b, s, 0))
    return pl.pallas_call(
        _gdn_in_kernel,
        grid=(B, S // tm),
        in_specs=[tile, halo, _resident((1, D)), _resident(wqkv.shape), _resident(conv_w.shape),
                  _resident(wz.shape), _resident(wab.shape), _resident(alog.shape), _resident(dtb.shape)],
        out_specs=[act_spec(qk), act_spec(qk), act_spec(nz), act_spec(nz), act_spec(LANES),
                   pl.BlockSpec((1, H, tm), lambda b, s: (b, 0, s))],
        out_shape=[act(qk), act(qk), act(nz), act(nz),
                   jax.ShapeDtypeStruct((B, S, LANES), F32), jax.ShapeDtypeStruct((B, H, S), F32)],
        scratch_shapes=[pltpu.VMEM((2, HALO + tm, MXU_COLS), F32)],
        compiler_params=_params("parallel", "parallel"),
        name="gdn_in",
    )(x, x, g[None, :], wqkv, conv_w, wz, wab, alog, dtb)


def _each(f, *lists):
    return [f(*args) for args in zip(*lists)]


def _same_block(ri, ci, size):
    shift = size.bit_length() - 1
    return (ri >> shift) == (ci >> shift)


def _unit_lower_inverses(a_s, ri, ci):
    eye = jnp.where(ri == ci, 1.0, 0.0)
    done = _each(lambda a: jnp.where(_same_block(ri, ci, SOLVE_BLOCK), a, 0.0), a_s)
    x = _each(lambda d: eye - d, done)
    power, order = done, 2
    while order < SOLVE_BLOCK:
        power = _each(_dot, power, power)
        x = _each(lambda x_, p: x_ + _dot(x_, p), x, power)
        order *= 2
    size = SOLVE_BLOCK
    while size < GDN_CHUNK:
        size *= 2
        part = a_s if size == GDN_CHUNK else _each(lambda a: jnp.where(_same_block(ri, ci, size), a, 0.0), a_s)
        t = _each(lambda p, d, x_: _dot(p - d, x_), part, done, x)
        x = _each(lambda x_, t_: x_ - _dot(x_, t_), x, t)
        done = part
    return x


def _gdn_chunk_kernel(q_ref, k_ref, v_ref, z_ref, gcol_ref, grow_ref, x_ref, onorm_ref, wout_ref,
                      o_ref, state_ref):
    R, C = GDN_ROWS, GDN_CHUNK
    ri = lax.broadcasted_iota(jnp.int32, (R, R), 0)
    ci = lax.broadcasted_iota(jnp.int32, (R, R), 1)
    same = _same_block(ri, ci, C)
    causal = same & (ri >= ci)
    strict = same & (ri > ci)
    gcol = gcol_ref[0]

    @pl.when(pl.program_id(1) == 0)
    def _():
        state_ref[...] = jnp.zeros(state_ref.shape, F32)

    heads = list(range(GDN_HEADS))
    head_lanes = [slice(h * LANES, (h + 1) * LANES) for h in heads]
    def l2_normed(t, scale):
        return t * (lax.rsqrt(jnp.sum(t * t, axis=-1, keepdims=True) + EPS) * scale)

    q = [l2_normed(_silu(q_ref[0, :, ln].astype(F32)), GDN_DK ** -0.5) for ln in head_lanes]
    k = [l2_normed(_silu(k_ref[0, :, ln].astype(F32)), 1.0) for ln in head_lanes]
    v = [_silu(v_ref[0, :, ln].astype(F32)) for ln in head_lanes]
    gc = [gcol[:, h:h + 1] for h in heads]
    beta = [gcol[:, GDN_HEADS + h:GDN_HEADS + h + 1] for h in heads]
    decay = [jnp.exp(jnp.where(causal, g - grow_ref[0, h:h + 1, :], -jnp.inf)) for g, h in zip(gc, heads)]
    kb = [k_.astype(BF16) for k_ in k]
    kk = _each(_dot_nt, kb, kb)
    a = _each(lambda kk_, d, b: jnp.where(strict, kk_ * d * b, 0.0), kk, decay, beta)
    tinv = _unit_lower_inverses(a, ri, ci)
    attn = _each(lambda q_, kb_, d: _dot_nt(q_, kb_) * d, q, kb, decay)
    egc = [jnp.exp(g) for g in gc]
    g_last = [jnp.concatenate([jnp.broadcast_to(g[c * C + C - 1:(c + 1) * C, :], (C, 1)) for c in range(R // C)],
                              axis=0) for g in gc]
    sol = _each(lambda t, v_, k_, b, e: _dot(t, jnp.concatenate([v_ * b, k_ * (b * e)], axis=1)),
                tinv, v, k, beta, egc)
    u = [s_[:, :LANES] for s_ in sol]
    w = [s_[:, LANES:] for s_ in sol]
    q_dec = _each(lambda q_, e: q_ * e, q, egc)
    k_dec = _each(lambda k_, gl, g: k_ * jnp.exp(gl - g), k, g_last, gc)

    state = [state_ref[h] for h in heads]
    corrected = [[] for _ in heads]
    from_state = [[] for _ in heads]
    for c in range(R // C):
        rows = slice(c * C, (c + 1) * C)
        r = _each(lambda w_, qd, st: _dot(jnp.concatenate([w_[rows], qd[rows]], axis=0), st), w, q_dec, state)
        uc = _each(lambda u_, r_: u_[rows] - r_[:C], u, r)
        for j in heads:
            corrected[j].append(uc[j])
            from_state[j].append(r[j][C:])
        state = _each(lambda st, gl, kd, uc_: st * jnp.exp(gl[c * C:c * C + 1, :]) + _dot_tn(kd[rows], uc_),
                      state, g_last, k_dec, uc)
    for h in heads:
        state_ref[h] = state[h]
    o = _each(lambda fs, at, co: jnp.concatenate(fs, axis=0) + _dot(at, jnp.concatenate(co, axis=0)),
              from_state, attn, corrected)
    o = _each(lambda o_, ln: _rms(o_, onorm_ref[...]) * _silu(z_ref[0, :, ln].astype(F32)), o, head_lanes)
    o_ref[0] = x_ref[0] + _dot(jnp.concatenate([o_.astype(BF16) for o_ in o], axis=1), wout_ref[...])


def _gdn_chunk(x, q, k, v, z, gcol, grow, out_norm, w_out):
    B, S, D = x.shape
    H, R = GDN_HEADS, GDN_ROWS
    dv = w_out.shape[0] // H
    row_block = lambda n: pl.BlockSpec((1, R, n), lambda b, s: (b, s, 0))
    return pl.pallas_call(
        _gdn_chunk_kernel,
        grid=(B, S // R),
        in_specs=[row_block(q.shape[2]), row_block(k.shape[2]), row_block(v.shape[2]), row_block(z.shape[2]),
                  row_block(LANES), pl.BlockSpec((1, H, R), lambda b, s: (b, 0, s)),
                  row_block(D), _resident((1, dv)), _resident(w_out.shape)],
        out_specs=row_block(D),
        out_shape=jax.ShapeDtypeStruct(x.shape, F32),
        scratch_shapes=[pltpu.VMEM((H, GDN_DK, dv), F32)],
        compiler_params=_params("parallel", "arbitrary"),
        name="gdn_chunk",
    )(q, k, v, z, gcol, grow, x, out_norm[None, :], w_out.astype(BF16))


def _proj_kernel(x_ref, g_ref, w_ref, gain_ref, o_ref, ot_ref=None, *, n_normed):
    h = _rms(x_ref[0], g_ref[...]).astype(BF16)
    width = o_ref.shape[2]
    for c in range(w_ref.shape[1] // MXU_COLS):
        y = _dot(h, w_ref[:, c * MXU_COLS:(c + 1) * MXU_COLS])
        for half in range(2):
            col = c * MXU_COLS + half * LANES
            yy = y[:, half * LANES:(half + 1) * LANES]
            if col < n_normed:
                lane = lax.broadcasted_iota(jnp.int32, yy.shape, 1)
                low = lane < DIFF_DH
                sq = yy * yy
                ms = jnp.where(low,
                               jnp.sum(jnp.where(low, sq, 0.0), axis=-1, keepdims=True),
                               jnp.sum(jnp.where(low, 0.0, sq), axis=-1, keepdims=True)) * (1.0 / DIFF_DH)
                yy = yy * lax.rsqrt(ms + EPS) * gain_ref[:, col:col + LANES]
            if col < width:
                o_ref[0, :, col:col + LANES] = yy.astype(o_ref.dtype)
            else:
                head = (col - width) // LANES
                ot_ref[0, head, 0, :LANES, :] = yy.T.astype(ot_ref.dtype)
                ot_ref[0, head, 0, LANES:, :] = jnp.ones((ONES_ROWS, yy.shape[0]), ot_ref.dtype)


def _proj(x, g, w, gain, *, tm):
    B, S, D = x.shape
    width = gain.shape[0]
    heads_t = (w.shape[1] - width) // LANES
    tile = pl.BlockSpec((1, tm, D), lambda b, s: (b, s, 0))
    out_specs = [pl.BlockSpec((1, tm, width), lambda b, s: (b, s, 0))]
    out_shape = [jax.ShapeDtypeStruct((B, S, width), BF16)]
    if heads_t:
        rows = LANES + ONES_ROWS
        out_specs.append(pl.BlockSpec((1, heads_t, 1, rows, tm), lambda b, s: (b, 0, s, 0, 0)))
        out_shape.append(jax.ShapeDtypeStruct((B, heads_t, S // tm, rows, tm), BF16))
    return pl.pallas_call(
        functools.partial(_proj_kernel, n_normed=width),
        grid=(B, S // tm),
        in_specs=[tile, _resident((1, D)), _resident(w.shape), _resident((1, width))],
        out_specs=out_specs,
        out_shape=out_shape,
        compiler_params=_params("parallel", "parallel"),
        name="norm_proj",
    )(x, g[None, :], w.astype(BF16), gain[None, :])


def _attn_kernel(lam_ref, q_ref, k_ref, vt_ref, sub_ref, o_ref, m_ref, acc_ref, sa_ref, sb_ref, *, lam_init):
    i = pl.program_id(2)
    tq, dv = o_ref.shape[1:]
    tk = vt_ref.shape[4]
    n = tq // tk
    assert n % 2 == 0
    q = q_ref[0]
    lane = lax.broadcasted_iota(jnp.int32, q.shape, 1)
    qs = (jnp.where(lane < DIFF_DH, q, jnp.zeros_like(q)), jnp.where(lane < DIFF_DH, jnp.zeros_like(q), q))
    m_ref[...] = jnp.full(m_ref.shape, -jnp.inf, F32)
    acc_ref[...] = jnp.zeros(acc_ref.shape, F32)

    def scores(j, dst_ref, q0=0):
        kb = k_ref[0, pl.ds(pl.multiple_of(j * tk, tk), tk), :]
        for c in range(2):
            dst_ref[c, :, q0:] = _dot_nt(kb, qs[c][q0:])

    def consume(src_ref, j, q0=0, q1=tq, masked=False):
        vt = vt_ref[0, 0, j]
        for c in range(2):
            sc = src_ref[c, :, q0:q1]
            if masked:
                key = lax.broadcasted_iota(jnp.int32, sc.shape, 0)
                qry = lax.broadcasted_iota(jnp.int32, sc.shape, 1)
                sc = jnp.where(key <= qry, sc, -jnp.inf)
            m_prev = m_ref[c, :, q0:q1]
            m_new = jnp.maximum(m_prev, jnp.max(sc, axis=0, keepdims=True))
            p = jnp.exp2(sc - m_new)
            acc_ref[c, :, q0:q1] = jnp.exp2(m_prev - m_new) * acc_ref[c, :, q0:q1] + _dot(vt, p)
            m_ref[c, :, q0:q1] = m_new

    scores(0, sa_ref)

    def two_blocks(pair, carry):
        j = 2 * pair
        scores(j + 1, sb_ref)
        consume(sa_ref, j)
        scores(j + 2, sa_ref)
        consume(sb_ref, j + 1)
        return carry

    lax.fori_loop(0, i * (n // 2), two_blocks, 0)
    bufs = (sa_ref, sb_ref)
    for d in range(n):
        if d + 1 < n:
            scores(n * i + d + 1, bufs[(d + 1) % 2], (d + 1) * tk)
        consume(bufs[d % 2], n * i + d, d * tk, (d + 1) * tk, masked=True)
        if d + 1 < n:
            consume(bufs[d % 2], n * i + d, (d + 1) * tk, tq)

    lam = lam_ref[...]
    lam_full = (jnp.exp(jnp.sum(lam[0:1] * lam[1:2], axis=-1, keepdims=True))
                - jnp.exp(jnp.sum(lam[2:3] * lam[3:4], axis=-1, keepdims=True)) + lam_init)
    o = (acc_ref[0, :dv] / acc_ref[0, dv:dv + 1] - lam_full * (acc_ref[1, :dv] / acc_ref[1, dv:dv + 1]))
    ms = jnp.mean(o * o, axis=0, keepdims=True)
    o = o * lax.rsqrt(ms + EPS) * (sub_ref[...] * (1.0 - lam_init))
    o_ref[0] = o.T.astype(o_ref.dtype)


def _diff_attention(q, k, vt, lam, subln, lam_init):
    B, S, _ = q.shape
    H = DIFF_HEADS
    nk, rows, tk = vt.shape[2:]
    tq = ATTN_QUERY_BLOCKS * tk
    dv = subln.shape[0]
    k_spec = pl.BlockSpec((1, S, LANES), lambda b, h, i: (b, 0, h))
    vt_spec = pl.BlockSpec((1, 1, nk, rows, tk), lambda b, h, i: (b, h, 0, 0, 0))
    q_spec = pl.BlockSpec((1, tq, LANES), lambda b, h, i: (b, i, h))
    return pl.pallas_call(
        functools.partial(_attn_kernel, lam_init=lam_init),
        grid=(B, H, S // tq),
        in_specs=[_resident(lam.shape), q_spec, k_spec, vt_spec, _resident((dv, 1))],
        out_specs=q_spec,
        out_shape=jax.ShapeDtypeStruct(q.shape, BF16),
        scratch_shapes=[pltpu.VMEM((2, 1, tq), F32), pltpu.VMEM((2, rows, tq), F32),
                        pltpu.VMEM((2, tk, tq), F32), pltpu.VMEM((2, tk, tq), F32)],
        compiler_params=_params("parallel", "parallel", "parallel"),
        name="diff_attention",
    )(lam, q, k, vt, subln[:, None])


def _out_proj_kernel(x_ref, o_ref, w_ref, y_ref):
    y_ref[0] = x_ref[0] + _dot(o_ref[0], w_ref[...])


def _out_proj(x, o, w, *, tm=1024):
    B, S, D = x.shape
    tile = lambda n: pl.BlockSpec((1, tm, n), lambda b, s: (b, s, 0))
    return pl.pallas_call(
        _out_proj_kernel,
        grid=(B, S // tm),
        in_specs=[tile(D), tile(o.shape[2]), _resident(w.shape)],
        out_specs=tile(D),
        out_shape=jax.ShapeDtypeStruct(x.shape, F32),
        compiler_params=_params("parallel", "parallel"),
        name="out_proj",
    )(x, o, w.astype(BF16))


def kernel(x, a_norm, a_w_in, a_conv_w, a_log, a_dt_bias, a_out_norm, a_w_out, kv_norm, kv_w, k_norm,
           b_norm, b_w_q, b_q_norm, b_lambda, b_subln, b_w_out, f_norm, f_w_up, f_conv_w, f_conv_b,
           f_w_down):
    n_a = a_norm.shape[0]
    depth = f_norm.shape[0]
    groups = DIFF_HEADS * 2
    a_w_in, a_w_out, kv_w, b_w_q, b_w_out, f_w_up, f_w_down = (
        w.astype(BF16) for w in (a_w_in, a_w_out, kv_w, b_w_q, b_w_out, f_w_up, f_w_down))
    k_sh = vt_sh = None
    for i in range(depth):
        if i < n_a:
            q, k, v, z, gcol, grow = _gdn_in(x, a_norm[i], a_w_in[i], a_conv_w[i], a_log[i], a_dt_bias[i])
            x = _gdn_chunk(x, q, k, v, z, gcol, grow, a_out_norm[i], a_w_out[i])
        else:
            if i == n_a:
                k_sh, vt_sh = _proj(x, kv_norm, kv_w, jnp.tile(k_norm, groups), tm=ATTN_BLOCK)
            j = i - n_a
            lam_init = 0.8 - 0.6 * math.exp(-0.3 * i)
            q_gain = jnp.tile(b_q_norm[j], groups) * (DIFF_DH ** -0.5 * LOG2E)
            (q,) = _proj(x, b_norm[j], b_w_q[j], q_gain, tm=2 * ATTN_BLOCK)
            o = _diff_attention(q, k_sh, vt_sh, b_lambda[j], b_subln[j], lam_init)
            x = _out_proj(x, o, b_w_out[j])
        x = _conv_ffn(x, f_norm[i], f_w_up[i], f_conv_w[i], f_conv_b[i], f_w_down[i])
    return x
```

```python
import functools
import math

import jax
import jax.numpy as jnp
from jax import lax
from jax.experimental import pallas as pl
from jax.experimental.pallas import tpu as pltpu

F32 = jnp.float32
BF16 = jnp.bfloat16

EPS = 1e-6
LANES = 128
MXU_COLS = 256
HALO = 16
GDN_HEADS = 8
GDN_DK = 128
GDN_CHUNK = 64
GDN_ROWS = 256
DIFF_HEADS = 8
DIFF_DH = 64
ATTN_BLOCK = 512
ATTN_QUERY_BLOCKS = 4
LOG2E = math.log2(math.e)
ONES_ROWS = 16
FF_COLS = MXU_COLS
FFN_AHEAD = 3
FFN_DOWN_GROUP = 4
SOLVE_BLOCK = 16
VMEM_LIMIT = 56 * 1024 * 1024


def _params(*sem):
    return pltpu.CompilerParams(dimension_semantics=sem, vmem_limit_bytes=VMEM_LIMIT)


def _resident(shape):
    zeros = (0,) * len(shape)
    return pl.BlockSpec(shape, lambda *_: zeros, pipeline_mode=pl.Buffered(1))


def _rms(x, g):
    ms = jnp.mean(x * x, axis=-1, keepdims=True)
    return x * lax.rsqrt(ms + EPS) * g


def _sigmoid(x):
    return 1.0 / (1.0 + jnp.exp(-x))


def _silu(x):
    half = 0.5 * x
    return half * (1.0 + jnp.tanh(half))


def _dot(a, b):
    return jnp.dot(a.astype(BF16), b.astype(BF16), preferred_element_type=F32)


def _dot_nt(a, b):
    return lax.dot_general(a.astype(BF16), b.astype(BF16), (((1,), (1,)), ((), ())),
                           preferred_element_type=F32)


def _dot_tn(a, b):
    return lax.dot_general(a.astype(BF16), b.astype(BF16), (((0,), (0,)), ((), ())),
                           preferred_element_type=F32)


def _normed_with_halo(x_ref, xh_ref, g_ref):
    g = g_ref[...]
    h = _rms(x_ref[0], g).astype(BF16)
    keep = (pl.program_id(1) > 0).astype(F32)
    hh = (_rms(xh_ref[0], g) * keep).astype(BF16)
    return h, jnp.concatenate([hh, h], axis=0)


def _causal_taps(rows, cw, tm, bias=None):
    taps = cw.shape[0]
    y = bias
    for d in range(taps):
        term = rows(HALO - d, HALO - d + tm) * cw[taps - 1 - d:taps - d]
        y = term if y is None else y + term
    return y


def _halo_specs(tm, d):
    tile = pl.BlockSpec((1, tm, d), lambda b, s: (b, s, 0))
    halo = pl.BlockSpec((1, HALO, d), lambda b, s: (b, jnp.maximum(s * (tm // HALO) - 1, 0), 0))
    return tile, halo


def _ffn_kernel(x_ref, g_ref, wup_ref, cw_ref, cb_ref, wd_ref, o_ref, u_ref, hist_ref, act_ref):
    h = _rms(x_ref[0], g_ref[...]).astype(BF16)
    tm = x_ref.shape[1]
    n_slots = u_ref.shape[0]
    ff = wd_ref.shape[0]
    n_slabs = ff // FF_COLS

    @pl.when(pl.program_id(1) == 0)
    def _():
        hist_ref[...] = jnp.zeros(hist_ref.shape, F32)

    def cols(c, part):
        return slice(part * ff + c * FF_COLS, part * ff + (c + 1) * FF_COLS)

    def up_proj(c):
        for part in range(2):
            u_ref[c % n_slots, part, :HALO] = hist_ref[c, part]
            u_ref[c % n_slots, part, HALO:] = _dot(h, wup_ref[:, cols(c, part)])

    def conv(c, part):
        y = _causal_taps(lambda a, b: u_ref[c % n_slots, part, a:b, :], cw_ref[:, cols(c, part)], tm,
                         cb_ref[:, cols(c, part)])
        hist_ref[c, part] = u_ref[c % n_slots, part, tm:, :]
        return y

    acc = x_ref[0]
    for c in range(min(FFN_AHEAD, n_slabs)):
        up_proj(c)
    for c in range(n_slabs):
        if c + FFN_AHEAD < n_slabs:
            up_proj(c + FFN_AHEAD)
        group, first = divmod(c, FFN_DOWN_GROUP)
        first *= FF_COLS
        act_ref[group % 2, :, first:first + FF_COLS] = (_silu(conv(c, 0)) * conv(c, 1)).astype(BF16)
        if (c + 1) % FFN_DOWN_GROUP == 0 or c + 1 == n_slabs:
            width = first + FF_COLS
            start = group * FFN_DOWN_GROUP * FF_COLS
            acc = acc + _dot(act_ref[group % 2, :, :width], wd_ref[start:start + width, :])
    o_ref[0] = acc


def _conv_ffn(x, g, w_up, conv_w, conv_b, w_down, *, tm=512):
    B, S, D = x.shape
    ff = w_down.shape[0]
    assert ff % FF_COLS == 0 and S % tm == 0
    tile = pl.BlockSpec((1, tm, D), lambda b, s: (b, s, 0))
    return pl.pallas_call(
        _ffn_kernel,
        grid=(B, S // tm),
        in_specs=[tile, _resident((1, D)), _resident(w_up.shape), _resident(conv_w.shape),
                  _resident((1, 2 * ff)), _resident(w_down.shape)],
        out_specs=tile,
        out_shape=jax.ShapeDtypeStruct(x.shape, F32),
        scratch_shapes=[pltpu.VMEM((FFN_AHEAD + 1, 2, HALO + tm, FF_COLS), F32),
                        pltpu.VMEM((ff // FF_COLS, 2, HALO, FF_COLS), F32),
                        pltpu.VMEM((2, tm, FFN_DOWN_GROUP * FF_COLS), BF16)],
        compiler_params=_params("parallel", "arbitrary"),
        name="conv_ffn",
    )(x, g[None, :], w_up.astype(BF16), conv_w, conv_b[None, :], w_down.astype(BF16))


def _chunk_cumsum(g):
    row = lax.broadcasted_iota(jnp.int32, g.shape, 0) & (GDN_CHUNK - 1)
    step = 1
    while step < GDN_CHUNK:
        g = g + jnp.where(row >= step, pltpu.roll(g, step, 0), 0.0)
        step *= 2
    return g


def _gdn_in_kernel(x_ref, xh_ref, g_ref, wqkv_ref, cw_ref, wz_ref, wab_ref, alog_ref, dtb_ref,
                   q_ref, k_ref, v_ref, z_ref, gcol_ref, grow_ref, u_ref):
    h, hext = _normed_with_halo(x_ref, xh_ref, g_ref)
    tm = x_ref.shape[1]
    qk_cols = GDN_HEADS * GDN_DK
    outs = (q_ref, k_ref, v_ref)
    n_chunks = wqkv_ref.shape[1] // MXU_COLS
    n_z = wz_ref.shape[1] // MXU_COLS
    assert n_z <= n_chunks and qk_cols % MXU_COLS == 0

    def tile_cols(c):
        return slice(c * MXU_COLS, (c + 1) * MXU_COLS)

    def project(c):
        u_ref[c % 2] = _dot(hext, wqkv_ref[:, tile_cols(c)])

    project(0)
    for c in range(n_chunks):
        cols = tile_cols(c)
        if c + 1 < n_chunks:
            project(c + 1)
        if (c + 1) * n_z // n_chunks > c * n_z // n_chunks:
            zc = tile_cols(c * n_z // n_chunks)
            z_ref[0, :, zc] = _dot(h, wz_ref[:, zc]).astype(z_ref.dtype)
        y = _causal_taps(lambda a, b: u_ref[c % 2, a:b, :], cw_ref[:, cols], tm)
        which, off = divmod(c * MXU_COLS, qk_cols)
        outs[which][0, :, off:off + MXU_COLS] = y.astype(outs[which].dtype)
    ab = _dot(h, wab_ref[...])
    t = ab + dtb_ref[...]
    softplus = jnp.maximum(t, 0.0) + jnp.log1p(jnp.exp(-jnp.abs(t)))
    cum = _chunk_cumsum(-jnp.exp(alog_ref[...]) * softplus)
    lane = lax.broadcasted_iota(jnp.int32, ab.shape, 1)
    gcol_ref[0] = jnp.where(lane < GDN_HEADS, cum, _sigmoid(ab))
    grow_ref[0] = cum.T[:GDN_HEADS, :]


def _gdn_in(x, g, w_in, conv_w, a_log, dt_bias, *, tm=512, act_dtype=BF16):
    B, S, D = x.shape
    H = GDN_HEADS
    qk = H * GDN_DK
    nqkv = conv_w.shape[1]
    nz = nqkv - 2 * qk
    wqkv = w_in[:, :nqkv].astype(BF16)
    wz = w_in[:, nqkv:nqkv + nz].astype(BF16)
    wab = jnp.pad(w_in[:, nqkv + nz:], ((0, 0), (0, LANES - 2 * H))).astype(BF16)
    alog = jnp.pad(a_log, (0, LANES - H))[None, :]
    dtb = jnp.pad(dt_bias, (0, LANES - H))[None, :]
    tile, halo = _halo_specs(tm, D)
    act = lambda n: jax.ShapeDtypeStruct((B, S, n), act_dtype)
    act_spec = lambda n: pl.BlockSpec((1, tm, n), lambda b, s: (b, s, 0))
    return pl.pallas_call(
        _gdn_in_kernel,
        grid=(B, S // tm),
        in_specs=[tile, halo, _resident((1, D)), _resident(wqkv.shape), _resident(conv_w.shape),
                  _resident(wz.shape), _resident(wab.shape), _resident(alog.shape), _resident(dtb.shape)],
        out_specs=[act_spec(qk), act_spec(qk), act_spec(nz), act_spec(nz), act_spec(LANES),
                   pl.BlockSpec((1, H, tm), lambda b, s: (b, 0, s))],
        out_shape=[act(qk), act(qk), act(nz), act(nz),
                   jax.ShapeDtypeStruct((B, S, LANES), F32), jax.ShapeDtypeStruct((B, H, S), F32)],
        scratch_shapes=[pltpu.VMEM((2, HALO + tm, MXU_COLS), F32)],
        compiler_params=_params("parallel", "parallel"),
        name="gdn_in",
    )(x, x, g[None, :], wqkv, conv_w, wz, wab, alog, dtb)


def _each(f, *lists):
    return [f(*args) for args in zip(*lists)]


def _same_block(ri, ci, size):
    shift = size.bit_length() - 1
    return (ri >> shift) == (ci >> shift)


def _unit_lower_inverses(a_s, ri, ci):
    eye = jnp.where(ri == ci, 1.0, 0.0)
    done = _each(lambda a: jnp.where(_same_block(ri, ci, SOLVE_BLOCK), a, 0.0), a_s)
    x = _each(lambda d: eye - d, done)
    power, order = done, 2
    while order < SOLVE_BLOCK:
        power = _each(_dot, power, power)
        x = _each(lambda x_, p: x_ + _dot(x_, p), x, power)
        order *= 2
    size = SOLVE_BLOCK
    while size < GDN_CHUNK:
        size *= 2
        part = a_s if size == GDN_CHUNK else _each(lambda a: jnp.where(_same_block(ri, ci, size), a, 0.0), a_s)
        t = _each(lambda p, d, x_: _dot(p - d, x_), part, done, x)
        x = _each(lambda x_, t_: x_ - _dot(x_, t_), x, t)
        done = part
    return x


def _gdn_chunk_kernel(q_ref, k_ref, v_ref, z_ref, gcol_ref, grow_ref, x_ref, onorm_ref, wout_ref,
                      o_ref, state_ref):
    R, C = GDN_ROWS, GDN_CHUNK
    ri = lax.broadcasted_iota(jnp.int32, (R, R), 0)
    ci = lax.broadcasted_iota(jnp.int32, (R, R), 1)
    same = _same_block(ri, ci, C)
    causal = same & (ri >= ci)
    strict = same & (ri > ci)
    gcol = gcol_ref[0]

    @pl.when(pl.program_id(1) == 0)
    def _():
        state_ref[...] = jnp.zeros(state_ref.shape, F32)

    heads = list(range(GDN_HEADS))
    head_lanes = [slice(h * LANES, (h + 1) * LANES) for h in heads]

    def l2_normed(t, scale):
        return t * (lax.rsqrt(jnp.sum(t * t, axis=-1, keepdims=True) + EPS) * scale)

    k = [l2_normed(_silu(k_ref[0, :, ln].astype(F32)), 1.0) for ln in head_lanes]
    kb = [k_.astype(BF16) for k_ in k]
    kk = _each(_dot_nt, kb, kb)
    gc = [gcol[:, h:h + 1] for h in heads]
    beta = [gcol[:, GDN_HEADS + h:GDN_HEADS + h + 1] for h in heads]
    decay = [jnp.exp(jnp.where(causal, g - grow_ref[0, h:h + 1, :], -jnp.inf)) for g, h in zip(gc, heads)]
    a = _each(lambda kk_, d, b: jnp.where(strict, kk_ * d * b, 0.0), kk, decay, beta)
    tinv = _unit_lower_inverses(a, ri, ci)
    q = [l2_normed(_silu(q_ref[0, :, ln].astype(F32)), GDN_DK ** -0.5) for ln in head_lanes]
    attn = _each(lambda q_, kb_, d: _dot_nt(q_, kb_) * d, q, kb, decay)
    v = [_silu(v_ref[0, :, ln].astype(F32)) for ln in head_lanes]
    egc = [jnp.exp(g) for g in gc]
    g_last = [jnp.concatenate([jnp.broadcast_to(g[c * C + C - 1:(c + 1) * C, :], (C, 1)) for c in range(R // C)],
                              axis=0) for g in gc]
    sol = _each(lambda t, v_, k_, b, e: _dot(t, jnp.concatenate([v_ * b, k_ * (b * e)], axis=1)),
                tinv, v, k, beta, egc)
    u = [s_[:, :LANES] for s_ in sol]
    w = [s_[:, LANES:] for s_ in sol]
    q_dec = _each(lambda q_, e: q_ * e, q, egc)
    k_dec = _each(lambda k_, gl, g: k_ * jnp.exp(gl - g), k, g_last, gc)

    state = [state_ref[h] for h in heads]
    corrected = [[] for _ in heads]
    from_state = [[] for _ in heads]
    for c in range(R // C):
        rows = slice(c * C, (c + 1) * C)
        r = _each(lambda w_, qd, st: _dot(jnp.concatenate([w_[rows], qd[rows]], axis=0), st), w, q_dec, state)
        uc = _each(lambda u_, r_: u_[rows] - r_[:C], u, r)
        for j in heads:
            corrected[j].append(uc[j])
            from_state[j].append(r[j][C:])
        state = _each(lambda st, gl, kd, uc_: st * jnp.exp(gl[c * C:c * C + 1, :]) + _dot_tn(kd[rows], uc_),
                      state, g_last, k_dec, uc)
    for h in heads:
        state_ref[h] = state[h]
    o = _each(lambda fs, at, co: jnp.concatenate(fs, axis=0) + _dot(at, jnp.concatenate(co, axis=0)),
              from_state, attn, corrected)
    o = _each(lambda o_, ln: _rms(o_, onorm_ref[...]) * _silu(z_ref[0, :, ln].astype(F32)), o, head_lanes)
    o_ref[0] = x_ref[0] + _dot(jnp.concatenate([o_.astype(BF16) for o_ in o], axis=1), wout_ref[...])


def _gdn_chunk(x, q, k, v, z, gcol, grow, out_norm, w_out):
    B, S, D = x.shape
    H, R = GDN_HEADS, GDN_ROWS
    dv = w_out.shape[0] // H
    row_block = lambda n: pl.BlockSpec((1, R, n), lambda b, s: (b, s, 0))
    return pl.pallas_call(
        _gdn_chunk_kernel,
        grid=(B, S // R),
        in_specs=[row_block(q.shape[2]), row_block(k.shape[2]), row_block(v.shape[2]), row_block(z.shape[2]),
                  row_block(LANES), pl.BlockSpec((1, H, R), lambda b, s: (b, 0, s)),
                  row_block(D), _resident((1, dv)), _resident(w_out.shape)],
        out_specs=row_block(D),
        out_shape=jax.ShapeDtypeStruct(x.shape, F32),
        scratch_shapes=[pltpu.VMEM((H, GDN_DK, dv), F32)],
        compiler_params=_params("parallel", "arbitrary"),
        name="gdn_chunk",
    )(q, k, v, z, gcol, grow, x, out_norm[None, :], w_out.astype(BF16))


def _proj_kernel(x_ref, g_ref, w_ref, gain_ref, o_ref, ot_ref=None, *, n_normed):
    h = _rms(x_ref[0], g_ref[...]).astype(BF16)
    width = o_ref.shape[2]
    for c in range(w_ref.shape[1] // MXU_COLS):
        y = _dot(h, w_ref[:, c * MXU_COLS:(c + 1) * MXU_COLS])
        for half in range(2):
            col = c * MXU_COLS + half * LANES
            yy = y[:, half * LANES:(half + 1) * LANES]
            if col < n_normed:
                lane = lax.broadcasted_iota(jnp.int32, yy.shape, 1)
                low = lane < DIFF_DH
                sq = yy * yy
                ms = jnp.where(low,
                               jnp.sum(jnp.where(low, sq, 0.0), axis=-1, keepdims=True),
                               jnp.sum(jnp.where(low, 0.0, sq), axis=-1, keepdims=True)) * (1.0 / DIFF_DH)
                yy = yy * lax.rsqrt(ms + EPS) * gain_ref[:, col:col + LANES]
            if col < width:
                o_ref[0, :, col:col + LANES] = yy.astype(o_ref.dtype)
            else:
                head = (col - width) // LANES
                ot_ref[0, head, 0, :LANES, :] = yy.T.astype(ot_ref.dtype)
                ot_ref[0, head, 0, LANES:, :] = jnp.ones((ONES_ROWS, yy.shape[0]), ot_ref.dtype)


def _proj(x, g, w, gain, *, tm):
    B, S, D = x.shape
    width = gain.shape[0]
    heads_t = (w.shape[1] - width) // LANES
    tile = pl.BlockSpec((1, tm, D), lambda b, s: (b, s, 0))
    out_specs = [pl.BlockSpec((1, tm, width), lambda b, s: (b, s, 0))]
    out_shape = [jax.ShapeDtypeStruct((B, S, width), BF16)]
    if heads_t:
        rows = LANES + ONES_ROWS
        out_specs.append(pl.BlockSpec((1, heads_t, 1, rows, tm), lambda b, s: (b, 0, s, 0, 0)))
        out_shape.append(jax.ShapeDtypeStruct((B, heads_t, S // tm, rows, tm), BF16))
    return pl.pallas_call(
        functools.partial(_proj_kernel, n_normed=width),
        grid=(B, S // tm),
        in_specs=[tile, _resident((1, D)), _resident(w.shape), _resident((1, width))],
        out_specs=out_specs,
        out_shape=out_shape,
        compiler_params=_params("parallel", "parallel"),
        name="norm_proj",
    )(x, g[None, :], w.astype(BF16), gain[None, :])


def _attn_kernel(lam_ref, q_ref, k_ref, vt_ref, sub_ref, o_ref, m_ref, acc_ref, sa_ref, sb_ref, *, lam_init):
    i = pl.program_id(2)
    tq, dv = o_ref.shape[1:]
    tk = vt_ref.shape[4]
    n = tq // tk
    assert n % 2 == 0
    q = q_ref[0]
    lane = lax.broadcasted_iota(jnp.int32, q.shape, 1)
    qs = (jnp.where(lane < DIFF_DH, q, jnp.zeros_like(q)), jnp.where(lane < DIFF_DH, jnp.zeros_like(q), q))
    m_ref[...] = jnp.full(m_ref.shape, -jnp.inf, F32)
    acc_ref[...] = jnp.zeros(acc_ref.shape, F32)

    def scores(j, dst_ref, q0=0):
        kb = k_ref[0, pl.ds(pl.multiple_of(j * tk, tk), tk), :]
        for c in range(2):
            dst_ref[c, :, q0:] = _dot_nt(kb, qs[c][q0:])

    def consume(src_ref, j, q0=0, q1=tq, masked=False):
        vt = vt_ref[0, 0, j]
        for c in range(2):
            sc = src_ref[c, :, q0:q1]
            if masked:
                key = lax.broadcasted_iota(jnp.int32, sc.shape, 0)
                qry = lax.broadcasted_iota(jnp.int32, sc.shape, 1)
                sc = jnp.where(key <= qry, sc, -jnp.inf)
            m_prev = m_ref[c, :, q0:q1]
            m_new = jnp.maximum(m_prev, jnp.max(sc, axis=0, keepdims=True))
            p = jnp.exp2(sc - m_new)
            acc_ref[c, :, q0:q1] = jnp.exp2(m_prev - m_new) * acc_ref[c, :, q0:q1] + _dot(vt, p)
            m_ref[c, :, q0:q1] = m_new

    scores(0, sa_ref)

    def two_blocks(pair, carry):
        j = 2 * pair
        scores(j + 1, sb_ref)
        consume(sa_ref, j)
        scores(j + 2, sa_ref)
        consume(sb_ref, j + 1)
        return carry

    lax.fori_loop(0, i * (n // 2), two_blocks, 0)
    bufs = (sa_ref, sb_ref)
    for d in range(n):
        if d + 1 < n:
            scores(n * i + d + 1, bufs[(d + 1) % 2], (d + 1) * tk)
        consume(bufs[d % 2], n * i + d, d * tk, (d + 1) * tk, masked=True)
        if d + 1 < n:
            consume(bufs[d % 2], n * i + d, (d + 1) * tk, tq)

    lam = lam_ref[...]
    lam_full = (jnp.exp(jnp.sum(lam[0:1] * lam[1:2], axis=-1, keepdims=True))
                - jnp.exp(jnp.sum(lam[2:3] * lam[3:4], axis=-1, keepdims=True)) + lam_init)
    o = (acc_ref[0, :dv] / acc_ref[0, dv:dv + 1] - lam_full * (acc_ref[1, :dv] / acc_ref[1, dv:dv + 1]))
    ms = jnp.mean(o * o, axis=0, keepdims=True)
    o = o * lax.rsqrt(ms + EPS) * (sub_ref[...] * (1.0 - lam_init))
    o_ref[0] = o.T.astype(o_ref.dtype)


def _diff_attention(q, k, vt, lam, subln, lam_init):
    B, S, _ = q.shape
    H = DIFF_HEADS
    nk, rows, tk = vt.shape[2:]
    tq = ATTN_QUERY_BLOCKS * tk
    dv = subln.shape[0]
    k_spec = pl.BlockSpec((1, S, LANES), lambda b, h, i: (b, 0, h))
    vt_spec = pl.BlockSpec((1, 1, nk, rows, tk), lambda b, h, i: (b, h, 0, 0, 0))
    q_spec = pl.BlockSpec((1, tq, LANES), lambda b, h, i: (b, i, h))
    return pl.pallas_call(
        functools.partial(_attn_kernel, lam_init=lam_init),
        grid=(B, H, S // tq),
        in_specs=[_resident(lam.shape), q_spec, k_spec, vt_spec, _resident((dv, 1))],
        out_specs=q_spec,
        out_shape=jax.ShapeDtypeStruct(q.shape, BF16),
        scratch_shapes=[pltpu.VMEM((2, 1, tq), F32), pltpu.VMEM((2, rows, tq), F32),
                        pltpu.VMEM((2, tk, tq), F32), pltpu.VMEM((2, tk, tq), F32)],
        compiler_params=_params("parallel", "parallel", "parallel"),
        name="diff_attention",
    )(lam, q, k, vt, subln[:, None])


def _out_proj_kernel(x_ref, o_ref, w_ref, y_ref):
    y_ref[0] = x_ref[0] + _dot(o_ref[0], w_ref[...])


def _out_proj(x, o, w, *, tm=1024):
    B, S, D = x.shape
    tile = lambda n: pl.BlockSpec((1, tm, n), lambda b, s: (b, s, 0))
    return pl.pallas_call(
        _out_proj_kernel,
        grid=(B, S // tm),
        in_specs=[tile(D), tile(o.shape[2]), _resident(w.shape)],
        out_specs=tile(D),
        out_shape=jax.ShapeDtypeStruct(x.shape, F32),
        compiler_params=_params("parallel", "parallel"),
        name="out_proj",
    )(x, o, w.astype(BF16))


def kernel(x, a_norm, a_w_in, a_conv_w, a_log, a_dt_bias, a_out_norm, a_w_out, kv_norm, kv_w, k_norm,
           b_norm, b_w_q, b_q_norm, b_lambda, b_subln, b_w_out, f_norm, f_w_up, f_conv_w, f_conv_b,
           f_w_down):
    n_a = a_norm.shape[0]
    depth = f_norm.shape[0]
    groups = DIFF_HEADS * 2
    a_w_in, a_w_out, kv_w, b_w_q, b_w_out, f_w_up, f_w_down = (
        w.astype(BF16) for w in (a_w_in, a_w_out, kv_w, b_w_q, b_w_out, f_w_up, f_w_down))
    k_sh = vt_sh = None
    for i in range(depth):
        if i < n_a:
            q, k, v, z, gcol, grow = _gdn_in(x, a_norm[i], a_w_in[i], a_conv_w[i], a_log[i], a_dt_bias[i])
            x = _gdn_chunk(x, q, k, v, z, gcol, grow, a_out_norm[i], a_w_out[i])
        else:
            if i == n_a:
                k_sh, vt_sh = _proj(x, kv_norm, kv_w, jnp.tile(k_norm, groups), tm=ATTN_BLOCK)
            j = i - n_a
            lam_init = 0.8 - 0.6 * math.exp(-0.3 * i)
            q_gain = jnp.tile(b_q_norm[j], groups) * (DIFF_DH ** -0.5 * LOG2E)
            (q,) = _proj(x, b_norm[j], b_w_q[j], q_gain, tm=2 * ATTN_BLOCK)
            o = _diff_attention(q, k_sh, vt_sh, b_lambda[j], b_subln[j], lam_init)
            x = _out_proj(x, o, b_w_out[j])
        x = _conv_ffn(x, f_norm[i], f_w_up[i], f_conv_w[i], f_conv_b[i], f_w_down[i])
    return x
```

```python
import functools
import math

import jax
import jax.numpy as jnp
from jax import lax
from jax.experimental import pallas as pl
from jax.experimental.pallas import tpu as pltpu

F32 = jnp.float32
BF16 = jnp.bfloat16

EPS = 1e-6
LANES = 128
MXU_COLS = 256
HALO = 16
GDN_HEADS = 8
GDN_DK = 128
GDN_CHUNK = 64
GDN_ROWS = 256
DIFF_HEADS = 8
DIFF_DH = 64
ATTN_BLOCK = 512
ATTN_QUERY_BLOCKS = 4
LOG2E = math.log2(math.e)
ONES_ROWS = 16
FF_COLS = MXU_COLS
FFN_AHEAD = 3
FFN_DOWN_GROUP = 4
SOLVE_BLOCK = 16
VMEM_LIMIT = 56 * 1024 * 1024


def _params(*sem):
    return pltpu.CompilerParams(dimension_semantics=sem, vmem_limit_bytes=VMEM_LIMIT)


def _resident(shape):
    zeros = (0,) * len(shape)
    return pl.BlockSpec(shape, lambda *_: zeros, pipeline_mode=pl.Buffered(1))


def _rms(x, g):
    ms = jnp.mean(x * x, axis=-1, keepdims=True)
    return x * lax.rsqrt(ms + EPS) * g


def _sigmoid(x):
    return 1.0 / (1.0 + jnp.exp(-x))


def _silu(x):
    half = 0.5 * x
    return half * (1.0 + jnp.tanh(half))


def _dot(a, b):
    return jnp.dot(a.astype(BF16), b.astype(BF16), preferred_element_type=F32)


def _dot_nt(a, b):
    return lax.dot_general(a.astype(BF16), b.astype(BF16), (((1,), (1,)), ((), ())),
                           preferred_element_type=F32)


def _dot_tn(a, b):
    return lax.dot_general(a.astype(BF16), b.astype(BF16), (((0,), (0,)), ((), ())),
                           preferred_element_type=F32)


def _normed_with_halo(x_ref, xh_ref, g_ref):
    g = g_ref[...]
    h = _rms(x_ref[0], g).astype(BF16)
    keep = (pl.program_id(1) > 0).astype(F32)
    hh = (_rms(xh_ref[0], g) * keep).astype(BF16)
    return h, jnp.concatenate([hh, h], axis=0)


def _causal_taps(rows, cw, tm, bias=None):
    taps = cw.shape[0]
    y = bias
    for d in range(taps):
        term = rows(HALO - d, HALO - d + tm) * cw[taps - 1 - d:taps - d]
        y = term if y is None else y + term
    return y


def _halo_specs(tm, d):
    tile = pl.BlockSpec((1, tm, d), lambda b, s: (b, s, 0))
    halo = pl.BlockSpec((1, HALO, d), lambda b, s: (b, jnp.maximum(s * (tm // HALO) - 1, 0), 0))
    return tile, halo


def _ffn_kernel(x_ref, g_ref, wup_ref, cw_ref, cb_ref, wd_ref, o_ref, u_ref, hist_ref, act_ref):
    h = _rms(x_ref[0], g_ref[...]).astype(BF16)
    tm = x_ref.shape[1]
    n_slots = u_ref.shape[0]
    ff = wd_ref.shape[0]
    n_slabs = ff // FF_COLS

    @pl.when(pl.program_id(1) == 0)
    def _():
        hist_ref[...] = jnp.zeros(hist_ref.shape, F32)

    def cols(c, part):
        return slice(part * ff + c * FF_COLS, part * ff + (c + 1) * FF_COLS)

    def up_proj(c):
        for part in range(2):
            u_ref[c % n_slots, part, :HALO] = hist_ref[c, part]
            u_ref[c % n_slots, part, HALO:] = _dot(h, wup_ref[:, cols(c, part)])

    def conv(c, part):
        y = _causal_taps(lambda a, b: u_ref[c % n_slots, part, a:b, :], cw_ref[:, cols(c, part)], tm,
                         cb_ref[:, cols(c, part)])
        hist_ref[c, part] = u_ref[c % n_slots, part, tm:, :]
        return y

    acc = x_ref[0]
    for c in range(min(FFN_AHEAD, n_slabs)):
        up_proj(c)
    for c in range(n_slabs):
        if c + FFN_AHEAD < n_slabs:
            up_proj(c + FFN_AHEAD)
        group, first = divmod(c, FFN_DOWN_GROUP)
        first *= FF_COLS
        act_ref[group % 2, :, first:first + FF_COLS] = (_silu(conv(c, 0)) * conv(c, 1)).astype(BF16)
        if (c + 1) % FFN_DOWN_GROUP == 0 or c + 1 == n_slabs:
            width = first + FF_COLS
            start = group * FFN_DOWN_GROUP * FF_COLS
            acc = acc + _dot(act_ref[group % 2, :, :width], wd_ref[start:start + width, :])
    o_ref[0] = acc


def _conv_ffn(x, g, w_up, conv_w, conv_b, w_down, *, tm=512):
    B, S, D = x.shape
    ff = w_down.shape[0]
    assert ff % FF_COLS == 0 and S % tm == 0
    tile = pl.BlockSpec((1, tm, D), lambda b, s: (b, s, 0))
    return pl.pallas_call(
        _ffn_kernel,
        grid=(B, S // tm),
        in_specs=[tile, _resident((1, D)), _resident(w_up.shape), _resident(conv_w.shape),
                  _resident((1, 2 * ff)), _resident(w_down.shape)],
        out_specs=tile,
        out_shape=jax.ShapeDtypeStruct(x.shape, F32),
        scratch_shapes=[pltpu.VMEM((FFN_AHEAD + 1, 2, HALO + tm, FF_COLS), F32),
                        pltpu.VMEM((ff // FF_COLS, 2, HALO, FF_COLS), F32),
                        pltpu.VMEM((2, tm, FFN_DOWN_GROUP * FF_COLS), BF16)],
        compiler_params=_params("parallel", "arbitrary"),
        name="conv_ffn",
    )(x, g[None, :], w_up.astype(BF16), conv_w, conv_b[None, :], w_down.astype(BF16))


def _chunk_cumsum(g):
    row = lax.broadcasted_iota(jnp.int32, g.shape, 0) & (GDN_CHUNK - 1)
    step = 1
    while step < GDN_CHUNK:
        g = g + jnp.where(row >= step, pltpu.roll(g, step, 0), 0.0)
        step *= 2
    return g


def _gdn_in_kernel(x_ref, xh_ref, g_ref, wqkv_ref, cw_ref, wz_ref, wab_ref, alog_ref, dtb_ref,
                   q_ref, k_ref, v_ref, z_ref, gcol_ref, grow_ref, u_ref):
    h, hext = _normed_with_halo(x_ref, xh_ref, g_ref)
    tm = x_ref.shape[1]
    qk_cols = GDN_HEADS * GDN_DK
    outs = (q_ref, k_ref, v_ref)
    n_chunks = wqkv_ref.shape[1] // MXU_COLS
    n_z = wz_ref.shape[1] // MXU_COLS
    assert n_z <= n_chunks and qk_cols % MXU_COLS == 0

    def tile_cols(c):
        return slice(c * MXU_COLS, (c + 1) * MXU_COLS)

    def project(c):
        u_ref[c % 2] = _dot(hext, wqkv_ref[:, tile_cols(c)])

    project(0)
    for c in range(n_chunks):
        cols = tile_cols(c)
        if c + 1 < n_chunks:
            project(c + 1)
        if (c + 1) * n_z // n_chunks > c * n_z // n_chunks:
            zc = tile_cols(c * n_z // n_chunks)
            z_ref[0, :, zc] = _dot(h, wz_ref[:, zc]).astype(z_ref.dtype)
        y = _causal_taps(lambda a, b: u_ref[c % 2, a:b, :], cw_ref[:, cols], tm)
        which, off = divmod(c * MXU_COLS, qk_cols)
        outs[which][0, :, off:off + MXU_COLS] = y.astype(outs[which].dtype)
    ab = _dot(h, wab_ref[...])
    t = ab + dtb_ref[...]
    softplus = jnp.maximum(t, 0.0) + jnp.log1p(jnp.exp(-jnp.abs(t)))
    cum = _chunk_cumsum(-jnp.exp(alog_ref[...]) * softplus)
    lane = lax.broadcasted_iota(jnp.int32, ab.shape, 1)
    gcol_ref[0] = jnp.where(lane < GDN_HEADS, cum, _sigmoid(ab))
    grow_ref[0] = cum.T[:GDN_HEADS, :]


def _gdn_in(x, g, w_in, conv_w, a_log, dt_bias, *, tm=512, act_dtype=BF16):
    B, S, D = x.shape
    H = GDN_HEADS
    qk = H * GDN_DK
    nqkv = conv_w.shape[1]
    nz = nqkv - 2 * qk
    wqkv = w_in[:, :nqkv].astype(BF16)
    wz = w_in[:, nqkv:nqkv + nz].astype(BF16)
    wab = jnp.pad(w_in[:, nqkv + nz:], ((0, 0), (0, LANES - 2 * H))).astype(BF16)
    alog = jnp.pad(a_log, (0, LANES - H))[None, :]
    dtb = jnp.pad(dt_bias, (0, LANES - H))[None, :]
    tile, halo = _halo_specs(tm, D)
    act = lambda n: jax.ShapeDtypeStruct((B, S, n), act_dtype)
    act_spec = lambda n: pl.BlockSpec((1, tm, n), lambda b, s: (b, s, 0))
    return pl.pallas_call(
        _gdn_in_kernel,
        grid=(B, S // tm),
        in_specs=[tile, halo, _resident((1, D)), _resident(wqkv.shape), _resident(conv_w.shape),
                  _resident(wz.shape), _resident(wab.shape), _resident(alog.shape), _resident(dtb.shape)],
        out_specs=[act_spec(qk), act_spec(qk), act_spec(nz), act_spec(nz), act_spec(LANES),
                   pl.BlockSpec((1, H, tm), lambda b, s: (b, 0, s))],
        out_shape=[act(qk), act(qk), act(nz), act(nz),
                   jax.ShapeDtypeStruct((B, S, LANES), F32), jax.ShapeDtypeStruct((B, H, S), F32)],
        scratch_shapes=[pltpu.VMEM((2, HALO + tm, MXU_COLS), F32)],
        compiler_params=_params("parallel", "parallel"),
        name="gdn_in",
    )(x, x, g[None, :], wqkv, conv_w, wz, wab, alog, dtb)


def _each(f, *lists):
    return [f(*args) for args in zip(*lists)]


def _same_block(ri, ci, size):
    shift = size.bit_length() - 1
    return (ri >> shift) == (ci >> shift)


def _unit_lower_inverses(a_s, ri, ci):
    eye = jnp.where(ri == ci, 1.0, 0.0)
    done = _each(lambda a: jnp.where(_same_block(ri, ci, SOLVE_BLOCK), a, 0.0), a_s)
    x = _each(lambda d: eye - d, done)
    power, order = done, 2
    while order < SOLVE_BLOCK:
        power = _each(_dot, power, power)
        x = _each(lambda x_, p: x_ + _dot(x_, p), x, power)
        order *= 2
    size = SOLVE_BLOCK
    while size < GDN_CHUNK:
        size *= 2
        part = a_s if size == GDN_CHUNK else _each(lambda a: jnp.where(_same_block(ri, ci, size), a, 0.0), a_s)
        t = _each(lambda p, d, x_: _dot(p - d, x_), part, done, x)
        x = _each(lambda x_, t_: x_ - _dot(x_, t_), x, t)
        done = part
    return x


def _gdn_chunk_kernel(q_ref, k_ref, v_ref, z_ref, gcol_ref, grow_ref, x_ref, onorm_ref, wout_ref,
                      o_ref, state_ref):
    R, C = GDN_ROWS, GDN_CHUNK
    ri = lax.broadcasted_iota(jnp.int32, (R, R), 0)
    ci = lax.broadcasted_iota(jnp.int32, (R, R), 1)
    same = _same_block(ri, ci, C)
    causal = same & (ri >= ci)
    strict = same & (ri > ci)
    gcol = gcol_ref[0]

    @pl.when(pl.program_id(1) == 0)
    def _():
        state_ref[...] = jnp.zeros(state_ref.shape, F32)

    heads = list(range(GDN_HEADS))
    head_lanes = [slice(h * LANES, (h + 1) * LANES) for h in heads]

    def l2_normed(t, scale):
        return t * (lax.rsqrt(jnp.sum(t * t, axis=-1, keepdims=True) + EPS) * scale)

    k = [l2_normed(_silu(k_ref[0, :, ln].astype(F32)), 1.0) for ln in head_lanes]
    kb = [k_.astype(BF16) for k_ in k]
    kk = _each(_dot_nt, kb, kb)
    gc = [gcol[:, h:h + 1] for h in heads]
    beta = [gcol[:, GDN_HEADS + h:GDN_HEADS + h + 1] for h in heads]
    decay = [jnp.exp(jnp.where(causal, g - grow_ref[0, h:h + 1, :], -jnp.inf)) for g, h in zip(gc, heads)]
    a = _each(lambda kk_, d, b: jnp.where(strict, kk_ * d * b, 0.0), kk, decay, beta)
    tinv = _unit_lower_inverses(a, ri, ci)
    q = [l2_normed(_silu(q_ref[0, :, ln].astype(F32)), GDN_DK ** -0.5) for ln in head_lanes]
    attn = _each(lambda q_, kb_, d: _dot_nt(q_, kb_) * d, q, kb, decay)
    v = [_silu(v_ref[0, :, ln].astype(F32)) for ln in head_lanes]
    egc = [jnp.exp(g) for g in gc]
    g_last = [jnp.concatenate([jnp.broadcast_to(g[c * C + C - 1:(c + 1) * C, :], (C, 1)) for c in range(R // C)],
                              axis=0) for g in gc]
    sol = _each(lambda t, v_, k_, b, e: _dot(t, jnp.concatenate([v_ * b, k_ * (b * e)], axis=1)),
                tinv, v, k, beta, egc)
    u = [s_[:, :LANES] for s_ in sol]
    w = [s_[:, LANES:] for s_ in sol]
    q_dec = _each(lambda q_, e: q_ * e, q, egc)
    k_dec = _each(lambda k_, gl, g: k_ * jnp.exp(gl - g), k, g_last, gc)

    state = [state_ref[h] for h in heads]
    corrected = [[] for _ in heads]
    from_state = [[] for _ in heads]
    for c in range(R // C):
        rows = slice(c * C, (c + 1) * C)
        r = _each(lambda w_, qd, st: _dot(jnp.concatenate([w_[rows], qd[rows]], axis=0), st), w, q_dec, state)
        uc = _each(lambda u_, r_: u_[rows] - r_[:C], u, r)
        for j in heads:
            corrected[j].append(uc[j])
            from_state[j].append(r[j][C:])
        state = _each(lambda st, gl, kd, uc_: st * jnp.exp(gl[c * C:c * C + 1, :]) + _dot_tn(kd[rows], uc_),
                      state, g_last, k_dec, uc)
    for h in heads:
        state_ref[h] = state[h]
    o = _each(lambda fs, at, co: jnp.concatenate(fs, axis=0) + _dot(at, jnp.concatenate(co, axis=0)),
              from_state, attn, corrected)
    o = _each(lambda o_, ln: _rms(o_, onorm_ref[...]) * _silu(z_ref[0, :, ln].astype(F32)), o, head_lanes)
    o_ref[0] = x_ref[0] + _dot(jnp.concatenate([o_.astype(BF16) for o_ in o], axis=1), wout_ref[...])


def _gdn_chunk(x, q, k, v, z, gcol, grow, out_norm, w_out):
    B, S, D = x.shape
    H, R = GDN_HEADS, GDN_ROWS
    dv = w_out.shape[0] // H
    row_block = lambda n: pl.BlockSpec((1, R, n), lambda b, s: (b, s, 0))
    return pl.pallas_call(
        _gdn_chunk_kernel,
        grid=(B, S // R),
        in_specs=[row_block(q.shape[2]), row_block(k.shape[2]), row_block(v.shape[2]), row_block(z.shape[2]),
                  row_block(LANES), pl.BlockSpec((1, H, R), lambda b, s: (b, 0, s)),
                  row_block(D), _resident((1, dv)), _resident(w_out.shape)],
        out_specs=row_block(D),
        out_shape=jax.ShapeDtypeStruct(x.shape, F32),
        scratch_shapes=[pltpu.VMEM((H, GDN_DK, dv), F32)],
        compiler_params=_params("parallel", "arbitrary"),
        name="gdn_chunk",
    )(q, k, v, z, gcol, grow, x, out_norm[None, :], w_out.astype(BF16))


def _proj_kernel(x_ref, g_ref, w_ref, gain_ref, o_ref, ot_ref=None, *, n_normed):
    h = _rms(x_ref[0], g_ref[...]).astype(BF16)
    width = o_ref.shape[2]
    for c in range(w_ref.shape[1] // MXU_COLS):
        y = _dot(h, w_ref[:, c * MXU_COLS:(c + 1) * MXU_COLS])
        for half in range(2):
            col = c * MXU_COLS + half * LANES
            yy = y[:, half * LANES:(half + 1) * LANES]
            if col < n_normed:
                lane = lax.broadcasted_iota(jnp.int32, yy.shape, 1)
                low = lane < DIFF_DH
                sq = yy * yy
                ms = jnp.where(low,
                               jnp.sum(jnp.where(low, sq, 0.0), axis=-1, keepdims=True),
                               jnp.sum(jnp.where(low, 0.0, sq), axis=-1, keepdims=True)) * (1.0 / DIFF_DH)
                yy = yy * lax.rsqrt(ms + EPS) * gain_ref[:, col:col + LANES]
            if col < width:
                o_ref[0, :, col:col + LANES] = yy.astype(o_ref.dtype)
            else:
                head = (col - width) // LANES
                ot_ref[0, head, 0, :LANES, :] = yy.T.astype(ot_ref.dtype)
                ot_ref[0, head, 0, LANES:, :] = jnp.ones((ONES_ROWS, yy.shape[0]), ot_ref.dtype)


def _proj(x, g, w, gain, *, tm):
    B, S, D = x.shape
    width = gain.shape[0]
    heads_t = (w.shape[1] - width) // LANES
    tile = pl.BlockSpec((1, tm, D), lambda b, s: (b, s, 0))
    out_specs = [pl.BlockSpec((1, tm, width), lambda b, s: (b, s, 0))]
    out_shape = [jax.ShapeDtypeStruct((B, S, width), BF16)]
    if heads_t:
        rows = LANES + ONES_ROWS
        out_specs.append(pl.BlockSpec((1, heads_t, 1, rows, tm), lambda b, s: (b, 0, s, 0, 0)))
        out_shape.append(jax.ShapeDtypeStruct((B, heads_t, S // tm, rows, tm), BF16))
    return pl.pallas_call(
        functools.partial(_proj_kernel, n_normed=width),
        grid=(B, S // tm),
        in_specs=[tile, _resident((1, D)), _resident(w.shape), _resident((1, width))],
        out_specs=out_specs,
        out_shape=out_shape,
        compiler_params=_params("parallel", "parallel"),
        name="norm_proj",
    )(x, g[None, :], w.astype(BF16), gain[None, :])


def _attn_kernel(lam_ref, q_ref, k_ref, vt_ref, sub_ref, o_ref, m_ref, acc_ref, sa_ref, sb_ref, ta_ref, tb_ref,
                 *, lam_init):
    i = pl.program_id(2)
    tq, dv = o_ref.shape[1:]
    tk = vt_ref.shape[4]
    n = tq // tk
    assert n % 2 == 0
    q = q_ref[0]
    lane = lax.broadcasted_iota(jnp.int32, q.shape, 1)
    qs = (jnp.where(lane < DIFF_DH, q, jnp.zeros_like(q)), jnp.where(lane < DIFF_DH, jnp.zeros_like(q), q))
    m_ref[...] = jnp.full(m_ref.shape, -jnp.inf, F32)
    acc_ref[...] = jnp.zeros(acc_ref.shape, F32)

    def scores(j, dst, q0=0):
        dst_ref, top_ref = dst
        kb = k_ref[0, pl.ds(pl.multiple_of(j * tk, tk), tk), :]
        for c in range(2):
            sc = _dot_nt(kb, qs[c][q0:])
            dst_ref[c, :, q0:] = sc
            top_ref[c, :, q0:] = jnp.max(sc, axis=0, keepdims=True)

    def consume(src, j, q0=0, q1=tq, masked=False):
        src_ref, top_ref = src
        vt = vt_ref[0, 0, j]
        for c in range(2):
            sc = src_ref[c, :, q0:q1]
            if masked:
                key = lax.broadcasted_iota(jnp.int32, sc.shape, 0)
                qry = lax.broadcasted_iota(jnp.int32, sc.shape, 1)
                sc = jnp.where(key <= qry, sc, -jnp.inf)
                top = jnp.max(sc, axis=0, keepdims=True)
            else:
                top = top_ref[c, :, q0:q1]
            m_prev = m_ref[c, :, q0:q1]
            m_new = jnp.maximum(m_prev, top)
            p = jnp.exp2(sc - m_new)
            acc_ref[c, :, q0:q1] = jnp.exp2(m_prev - m_new) * acc_ref[c, :, q0:q1] + _dot(vt, p)
            m_ref[c, :, q0:q1] = m_new

    bufs = ((sa_ref, ta_ref), (sb_ref, tb_ref))
    scores(0, bufs[0])

    def two_blocks(pair, carry):
        j = 2 * pair
        scores(j + 1, bufs[1])
        consume(bufs[0], j)
        scores(j + 2, bufs[0])
        consume(bufs[1], j + 1)
        return carry

    lax.fori_loop(0, i * (n // 2), two_blocks, 0)
    for d in range(n):
        if d + 1 < n:
            scores(n * i + d + 1, bufs[(d + 1) % 2], (d + 1) * tk)
        consume(bufs[d % 2], n * i + d, d * tk, (d + 1) * tk, masked=True)
        if d + 1 < n:
            consume(bufs[d % 2], n * i + d, (d + 1) * tk, tq)

    lam = lam_ref[...]
    lam_full = (jnp.exp(jnp.sum(lam[0:1] * lam[1:2], axis=-1, keepdims=True))
                - jnp.exp(jnp.sum(lam[2:3] * lam[3:4], axis=-1, keepdims=True)) + lam_init)
    o = (acc_ref[0, :dv] / acc_ref[0, dv:dv + 1] - lam_full * (acc_ref[1, :dv] / acc_ref[1, dv:dv + 1]))
    ms = jnp.mean(o * o, axis=0, keepdims=True)
    o = o * lax.rsqrt(ms + EPS) * (sub_ref[...] * (1.0 - lam_init))
    o_ref[0] = o.T.astype(o_ref.dtype)


def _diff_attention(q, k, vt, lam, subln, lam_init):
    B, S, _ = q.shape
    H = DIFF_HEADS
    nk, rows, tk = vt.shape[2:]
    tq = ATTN_QUERY_BLOCKS * tk
    dv = subln.shape[0]
    k_spec = pl.BlockSpec((1, S, LANES), lambda b, h, i: (b, 0, h))
    vt_spec = pl.BlockSpec((1, 1, nk, rows, tk), lambda b, h, i: (b, h, 0, 0, 0))
    q_spec = pl.BlockSpec((1, tq, LANES), lambda b, h, i: (b, i, h))
    return pl.pallas_call(
        functools.partial(_attn_kernel, lam_init=lam_init),
        grid=(B, H, S // tq),
        in_specs=[_resident(lam.shape), q_spec, k_spec, vt_spec, _resident((dv, 1))],
        out_specs=q_spec,
        out_shape=jax.ShapeDtypeStruct(q.shape, BF16),
        scratch_shapes=[pltpu.VMEM((2, 1, tq), F32), pltpu.VMEM((2, rows, tq), F32),
                        pltpu.VMEM((2, tk, tq), F32), pltpu.VMEM((2, tk, tq), F32),
                        pltpu.VMEM((2, 1, tq), F32), pltpu.VMEM((2, 1, tq), F32)],
        compiler_params=_params("parallel", "parallel", "parallel"),
        name="diff_attention",
    )(lam, q, k, vt, subln[:, None])


def _out_proj_kernel(x_ref, o_ref, w_ref, y_ref):
    y_ref[0] = x_ref[0] + _dot(o_ref[0], w_ref[...])


def _out_proj(x, o, w, *, tm=1024):
    B, S, D = x.shape
    tile = lambda n: pl.BlockSpec((1, tm, n), lambda b, s: (b, s, 0))
    return pl.pallas_call(
        _out_proj_kernel,
        grid=(B, S // tm),
        in_specs=[tile(D), tile(o.shape[2]), _resident(w.shape)],
        out_specs=tile(D),
        out_shape=jax.ShapeDtypeStruct(x.shape, F32),
        compiler_params=_params("parallel", "parallel"),
        name="out_proj",
    )(x, o, w.astype(BF16))


def kernel(x, a_norm, a_w_in, a_conv_w, a_log, a_dt_bias, a_out_norm, a_w_out, kv_norm, kv_w, k_norm,
           b_norm, b_w_q, b_q_norm, b_lambda, b_subln, b_w_out, f_norm, f_w_up, f_conv_w, f_conv_b,
           f_w_down):
    n_a = a_norm.shape[0]
    depth = f_norm.shape[0]
    groups = DIFF_HEADS * 2
    a_w_in, a_w_out, kv_w, b_w_q, b_w_out, f_w_up, f_w_down = (
        w.astype(BF16) for w in (a_w_in, a_w_out, kv_w, b_w_q, b_w_out, f_w_up, f_w_down))
    k_sh = vt_sh = None
    for i in range(depth):
        if i < n_a:
            q, k, v, z, gcol, grow = _gdn_in(x, a_norm[i], a_w_in[i], a_conv_w[i], a_log[i], a_dt_bias[i])
            x = _gdn_chunk(x, q, k, v, z, gcol, grow, a_out_norm[i], a_w_out[i])
        else:
            if i == n_a:
                k_sh, vt_sh = _proj(x, kv_norm, kv_w, jnp.tile(k_norm, groups), tm=ATTN_BLOCK)
            j = i - n_a
            lam_init = 0.8 - 0.6 * math.exp(-0.3 * i)
            q_gain = jnp.tile(b_q_norm[j], groups) * (DIFF_DH ** -0.5 * LOG2E)
            (q,) = _proj(x, b_norm[j], b_w_q[j], q_gain, tm=2 * ATTN_BLOCK)
            o = _diff_attention(q, k_sh, vt_sh, b_lambda[j], b_subln[j], lam_init)
            x = _out_proj(x, o, b_w_out[j])
        x = _conv_ffn(x, f_norm[i], f_w_up[i], f_conv_w[i], f_conv_b[i], f_w_down[i])
    return x
```

```python
import functools
import math

import jax
import jax.numpy as jnp
from jax import lax
from jax.experimental import pallas as pl
from jax.experimental.pallas import tpu as pltpu

F32 = jnp.float32
BF16 = jnp.bfloat16

EPS = 1e-6
LANES = 128
MXU_COLS = 256
HALO = 16
GDN_HEADS = 8
GDN_DK = 128
GDN_CHUNK = 64
GDN_ROWS = 256
DIFF_HEADS = 8
DIFF_DH = 64
ATTN_BLOCK = 512
ATTN_QUERY_BLOCKS = 4
LOG2E = math.log2(math.e)
ONES_ROWS = 16
FF_COLS = MXU_COLS
FFN_AHEAD = 3
FFN_DOWN_GROUP = 4
SOLVE_BLOCK = 16
VMEM_LIMIT = 56 * 1024 * 1024


def _params(*sem):
    return pltpu.CompilerParams(dimension_semantics=sem, vmem_limit_bytes=VMEM_LIMIT)


def _resident(shape):
    zeros = (0,) * len(shape)
    return pl.BlockSpec(shape, lambda *_: zeros, pipeline_mode=pl.Buffered(1))


def _rms(x, g):
    ms = jnp.mean(x * x, axis=-1, keepdims=True)
    return x * lax.rsqrt(ms + EPS) * g


def _sigmoid(x):
    return 1.0 / (1.0 + jnp.exp(-x))


def _silu(x):
    half = 0.5 * x
    return half * (1.0 + jnp.tanh(half))


def _dot(a, b):
    return jnp.dot(a.astype(BF16), b.astype(BF16), preferred_element_type=F32)


def _dot_nt(a, b):
    return lax.dot_general(a.astype(BF16), b.astype(BF16), (((1,), (1,)), ((), ())),
                           preferred_element_type=F32)


def _dot_tn(a, b):
    return lax.dot_general(a.astype(BF16), b.astype(BF16), (((0,), (0,)), ((), ())),
                           preferred_element_type=F32)


def _normed_with_halo(x_ref, xh_ref, g_ref):
    g = g_ref[...]
    h = _rms(x_ref[0], g).astype(BF16)
    keep = (pl.program_id(1) > 0).astype(F32)
    hh = (_rms(xh_ref[0], g) * keep).astype(BF16)
    return h, jnp.concatenate([hh, h], axis=0)


def _causal_taps(rows, cw, tm, bias=None):
    taps = cw.shape[0]
    y = bias
    for d in range(taps):
        term = rows(HALO - d, HALO - d + tm) * cw[taps - 1 - d:taps - d]
        y = term if y is None else y + term
    return y


def _halo_specs(tm, d):
    tile = pl.BlockSpec((1, tm, d), lambda b, s: (b, s, 0))
    halo = pl.BlockSpec((1, HALO, d), lambda b, s: (b, jnp.maximum(s * (tm // HALO) - 1, 0), 0))
    return tile, halo


def _ffn_kernel(x_ref, g_ref, wup_ref, cw_ref, cb_ref, wd_ref, o_ref, u_ref, hist_ref, act_ref):
    h = _rms(x_ref[0], g_ref[...]).astype(BF16)
    tm = x_ref.shape[1]
    n_slots = u_ref.shape[0]
    ff = wd_ref.shape[0]
    n_slabs = ff // FF_COLS

    @pl.when(pl.program_id(1) == 0)
    def _():
        hist_ref[...] = jnp.zeros(hist_ref.shape, F32)

    def cols(c, part):
        return slice(part * ff + c * FF_COLS, part * ff + (c + 1) * FF_COLS)

    def up_proj(c):
        for part in range(2):
            u_ref[c % n_slots, part, :HALO] = hist_ref[c, part]
            u_ref[c % n_slots, part, HALO:] = _dot(h, wup_ref[:, cols(c, part)])

    def conv(c, part):
        y = _causal_taps(lambda a, b: u_ref[c % n_slots, part, a:b, :], cw_ref[:, cols(c, part)], tm,
                         cb_ref[:, cols(c, part)])
        hist_ref[c, part] = u_ref[c % n_slots, part, tm:, :]
        return y

    acc = x_ref[0]
    for c in range(min(FFN_AHEAD, n_slabs)):
        up_proj(c)
    for c in range(n_slabs):
        if c + FFN_AHEAD < n_slabs:
            up_proj(c + FFN_AHEAD)
        group, first = divmod(c, FFN_DOWN_GROUP)
        first *= FF_COLS
        act_ref[group % 2, :, first:first + FF_COLS] = (_silu(conv(c, 0)) * conv(c, 1)).astype(BF16)
        if (c + 1) % FFN_DOWN_GROUP == 0 or c + 1 == n_slabs:
            width = first + FF_COLS
            start = group * FFN_DOWN_GROUP * FF_COLS
            acc = acc + _dot(act_ref[group % 2, :, :width], wd_ref[start:start + width, :])
    o_ref[0] = acc


def _conv_ffn(x, g, w_up, conv_w, conv_b, w_down, *, tm=512):
    B, S, D = x.shape
    ff = w_down.shape[0]
    assert ff % FF_COLS == 0 and S % tm == 0
    tile = pl.BlockSpec((1, tm, D), lambda b, s: (b, s, 0))
    return pl.pallas_call(
        _ffn_kernel,
        grid=(B, S // tm),
        in_specs=[tile, _resident((1, D)), _resident(w_up.shape), _resident(conv_w.shape),
                  _resident((1, 2 * ff)), _resident(w_down.shape)],
        out_specs=tile,
        out_shape=jax.ShapeDtypeStruct(x.shape, F32),
        scratch_shapes=[pltpu.VMEM((FFN_AHEAD + 1, 2, HALO + tm, FF_COLS), F32),
                        pltpu.VMEM((ff // FF_COLS, 2, HALO, FF_COLS), F32),
                        pltpu.VMEM((2, tm, FFN_DOWN_GROUP * FF_COLS), BF16)],
        compiler_params=_params("parallel", "arbitrary"),
        name="conv_ffn",
    )(x, g[None, :], w_up.astype(BF16), conv_w, conv_b[None, :], w_down.astype(BF16))


def _chunk_cumsum(g):
    row = lax.broadcasted_iota(jnp.int32, g.shape, 0) & (GDN_CHUNK - 1)
    step = 1
    while step < GDN_CHUNK:
        g = g + jnp.where(row >= step, pltpu.roll(g, step, 0), 0.0)
        step *= 2
    return g


def _gdn_in_kernel(x_ref, xh_ref, g_ref, wqkv_ref, cw_ref, wz_ref, wab_ref, alog_ref, dtb_ref,
                   q_ref, k_ref, v_ref, z_ref, gcol_ref, grow_ref, u_ref):
    h, hext = _normed_with_halo(x_ref, xh_ref, g_ref)
    tm = x_ref.shape[1]
    qk_cols = GDN_HEADS * GDN_DK
    outs = (q_ref, k_ref, v_ref)
    n_chunks = wqkv_ref.shape[1] // MXU_COLS
    n_z = wz_ref.shape[1] // MXU_COLS
    assert n_z <= n_chunks and qk_cols % MXU_COLS == 0

    def tile_cols(c):
        return slice(c * MXU_COLS, (c + 1) * MXU_COLS)

    def project(c):
        u_ref[c % 2] = _dot(hext, wqkv_ref[:, tile_cols(c)])

    project(0)
    for c in range(n_chunks):
        cols = tile_cols(c)
        if c + 1 < n_chunks:
            project(c + 1)
        if (c + 1) * n_z // n_chunks > c * n_z // n_chunks:
            zc = tile_cols(c * n_z // n_chunks)
            z_ref[0, :, zc] = _dot(h, wz_ref[:, zc]).astype(z_ref.dtype)
        y = _causal_taps(lambda a, b: u_ref[c % 2, a:b, :], cw_ref[:, cols], tm)
        which, off = divmod(c * MXU_COLS, qk_cols)
        outs[which][0, :, off:off + MXU_COLS] = y.astype(outs[which].dtype)
    ab = _dot(h, wab_ref[...])
    t = ab + dtb_ref[...]
    softplus = jnp.maximum(t, 0.0) + jnp.log1p(jnp.exp(-jnp.abs(t)))
    cum = _chunk_cumsum(-jnp.exp(alog_ref[...]) * softplus)
    lane = lax.broadcasted_iota(jnp.int32, ab.shape, 1)
    gcol_ref[0] = jnp.where(lane < GDN_HEADS, cum, _sigmoid(ab))
    grow_ref[0] = cum.T[:GDN_HEADS, :]


def _gdn_in(x, g, w_in, conv_w, a_log, dt_bias, *, tm=512, act_dtype=BF16):
    B, S, D = x.shape
    H = GDN_HEADS
    qk = H * GDN_DK
    nqkv = conv_w.shape[1]
    nz = nqkv - 2 * qk
    wqkv = w_in[:, :nqkv].astype(BF16)
    wz = w_in[:, nqkv:nqkv + nz].astype(BF16)
    wab = jnp.pad(w_in[:, nqkv + nz:], ((0, 0), (0, LANES - 2 * H))).astype(BF16)
    alog = jnp.pad(a_log, (0, LANES - H))[None, :]
    dtb = jnp.pad(dt_bias, (0, LANES - H))[None, :]
    tile, halo = _halo_specs(tm, D)
    act = lambda n: jax.ShapeDtypeStruct((B, S, n), act_dtype)
    act_spec = lambda n: pl.BlockSpec((1, tm, n), lambda b, s: (b, s, 0))
    return pl.pallas_call(
        _gdn_in_kernel,
        grid=(B, S // tm),
        in_specs=[tile, halo, _resident((1, D)), _resident(wqkv.shape), _resident(conv_w.shape),
                  _resident(wz.shape), _resident(wab.shape), _resident(alog.shape), _resident(dtb.shape)],
        out_specs=[act_spec(qk), act_spec(qk), act_spec(nz), act_spec(nz), act_spec(LANES),
                   pl.BlockSpec((1, H, tm), lambda b, s: (b, 0, s))],
        out_shape=[act(qk), act(qk), act(nz), act(nz),
                   jax.ShapeDtypeStruct((B, S, LANES), F32), jax.ShapeDtypeStruct((B, H, S), F32)],
        scratch_shapes=[pltpu.VMEM((2, HALO + tm, MXU_COLS), F32)],
        compiler_params=_params("parallel", "parallel"),
        name="gdn_in",
    )(x, x, g[None, :], wqkv, conv_w, wz, wab, alog, dtb)


def _each(f, *lists):
    return [f(*args) for args in zip(*lists)]


def _same_block(ri, ci, size):
    shift = size.bit_length() - 1
    return (ri >> shift) == (ci >> shift)


def _unit_lower_inverses(a_s, ri, ci):
    eye = jnp.where(ri == ci, 1.0, 0.0)
    done = _each(lambda a: jnp.where(_same_block(ri, ci, SOLVE_BLOCK), a, 0.0), a_s)
    x = _each(lambda d: eye - d, done)
    power, order = done, 2
    while order < SOLVE_BLOCK:
        power = _each(_dot, power, power)
        x = _each(lambda x_, p: x_ + _dot(x_, p), x, power)
        order *= 2
    size = SOLVE_BLOCK
    while size < GDN_CHUNK:
        size *= 2
        part = a_s if size == GDN_CHUNK else _each(lambda a: jnp.where(_same_block(ri, ci, size), a, 0.0), a_s)
        t = _each(lambda p, d, x_: _dot(p - d, x_), part, done, x)
        x = _each(lambda x_, t_: x_ - _dot(x_, t_), x, t)
        done = part
    return x


def _gdn_chunk_kernel(q_ref, k_ref, v_ref, z_ref, gcol_ref, grow_ref, x_ref, onorm_ref, wout_ref,
                      o_ref, state_ref):
    R, C = GDN_ROWS, GDN_CHUNK
    ri = lax.broadcasted_iota(jnp.int32, (R, R), 0)
    ci = lax.broadcasted_iota(jnp.int32, (R, R), 1)
    same = _same_block(ri, ci, C)
    causal = same & (ri >= ci)
    strict = same & (ri > ci)
    gcol = gcol_ref[0]

    @pl.when(pl.program_id(1) == 0)
    def _():
        state_ref[...] = jnp.zeros(state_ref.shape, F32)

    heads = list(range(GDN_HEADS))
    head_lanes = [slice(h * LANES, (h + 1) * LANES) for h in heads]

    def l2_normed(t, scale):
        return t * (lax.rsqrt(jnp.sum(t * t, axis=-1, keepdims=True) + EPS) * scale)

    k = [l2_normed(_silu(k_ref[0, :, ln].astype(F32)), 1.0) for ln in head_lanes]
    kb = [k_.astype(BF16) for k_ in k]
    kk = _each(_dot_nt, kb, kb)
    gc = [gcol[:, h:h + 1] for h in heads]
    beta = [gcol[:, GDN_HEADS + h:GDN_HEADS + h + 1] for h in heads]
    decay = [jnp.exp(jnp.where(causal, g - grow_ref[0, h:h + 1, :], -jnp.inf)) for g, h in zip(gc, heads)]
    a = _each(lambda kk_, d, b: jnp.where(strict, kk_ * d * b, 0.0), kk, decay, beta)
    tinv = _unit_lower_inverses(a, ri, ci)
    q = [l2_normed(_silu(q_ref[0, :, ln].astype(F32)), GDN_DK ** -0.5) for ln in head_lanes]
    attn = _each(lambda q_, kb_, d: _dot_nt(q_, kb_) * d, q, kb, decay)
    v = [_silu(v_ref[0, :, ln].astype(F32)) for ln in head_lanes]
    egc = [jnp.exp(g) for g in gc]
    g_last = [jnp.concatenate([jnp.broadcast_to(g[c * C + C - 1:(c + 1) * C, :], (C, 1)) for c in range(R // C)],
                              axis=0) for g in gc]
    sol = _each(lambda t, v_, k_, b, e: _dot(t, jnp.concatenate([v_ * b, k_ * (b * e)], axis=1)),
                tinv, v, k, beta, egc)
    u = [s_[:, :LANES] for s_ in sol]
    w = [s_[:, LANES:] for s_ in sol]
    q_dec = _each(lambda q_, e: q_ * e, q, egc)
    k_dec = _each(lambda k_, gl, g: k_ * jnp.exp(gl - g), k, g_last, gc)

    state = [state_ref[h] for h in heads]
    corrected = [[] for _ in heads]
    from_state = [[] for _ in heads]
    for c in range(R // C):
        rows = slice(c * C, (c + 1) * C)
        r = _each(lambda w_, qd, st: _dot(jnp.concatenate([w_[rows], qd[rows]], axis=0), st), w, q_dec, state)
        uc = _each(lambda u_, r_: u_[rows] - r_[:C], u, r)
        for j in heads:
            corrected[j].append(uc[j])
            from_state[j].append(r[j][C:])
        state = _each(lambda st, gl, kd, uc_: st * jnp.exp(gl[c * C:c * C + 1, :]) + _dot_tn(kd[rows], uc_),
                      state, g_last, k_dec, uc)
    for h in heads:
        state_ref[h] = state[h]
    o = _each(lambda fs, at, co: jnp.concatenate(fs, axis=0) + _dot(at, jnp.concatenate(co, axis=0)),
              from_state, attn, corrected)
    o = _each(lambda o_, ln: _rms(o_, onorm_ref[...]) * _silu(z_ref[0, :, ln].astype(F32)), o, head_lanes)
    o_ref[0] = x_ref[0] + _dot(jnp.concatenate([o_.astype(BF16) for o_ in o], axis=1), wout_ref[...])


def _gdn_chunk(x, q, k, v, z, gcol, grow, out_norm, w_out):
    B, S, D = x.shape
    H, R = GDN_HEADS, GDN_ROWS
    dv = w_out.shape[0] // H
    row_block = lambda n: pl.BlockSpec((1, R, n), lambda b, s: (b, s, 0))
    return pl.pallas_call(
        _gdn_chunk_kernel,
        grid=(B, S // R),
        in_specs=[row_block(q.shape[2]), row_block(k.shape[2]), row_block(v.shape[2]), row_block(z.shape[2]),
                  row_block(LANES), pl.BlockSpec((1, H, R), lambda b, s: (b, 0, s)),
                  row_block(D), _resident((1, dv)), _resident(w_out.shape)],
        out_specs=row_block(D),
        out_shape=jax.ShapeDtypeStruct(x.shape, F32),
        scratch_shapes=[pltpu.VMEM((H, GDN_DK, dv), F32)],
        compiler_params=_params("parallel", "arbitrary"),
        name="gdn_chunk",
    )(q, k, v, z, gcol, grow, x, out_norm[None, :], w_out.astype(BF16))


def _proj_kernel(x_ref, g_ref, w_ref, gain_ref, o_ref, ot_ref=None, *, n_normed):
    h = _rms(x_ref[0], g_ref[...]).astype(BF16)
    width = o_ref.shape[2]
    for c in range(w_ref.shape[1] // MXU_COLS):
        y = _dot(h, w_ref[:, c * MXU_COLS:(c + 1) * MXU_COLS])
        for half in range(2):
            col = c * MXU_COLS + half * LANES
            yy = y[:, half * LANES:(half + 1) * LANES]
            if col < n_normed:
                lane = lax.broadcasted_iota(jnp.int32, yy.shape, 1)
                low = lane < DIFF_DH
                sq = yy * yy
                ms = jnp.where(low,
                               jnp.sum(jnp.where(low, sq, 0.0), axis=-1, keepdims=True),
                               jnp.sum(jnp.where(low, 0.0, sq), axis=-1, keepdims=True)) * (1.0 / DIFF_DH)
                yy = yy * lax.rsqrt(ms + EPS) * gain_ref[:, col:col + LANES]
            if col < width:
                o_ref[0, :, col:col + LANES] = yy.astype(o_ref.dtype)
            else:
                head = (col - width) // LANES
                ot_ref[0, head, 0, :LANES, :] = yy.T.astype(ot_ref.dtype)
                ot_ref[0, head, 0, LANES:, :] = jnp.ones((ONES_ROWS, yy.shape[0]), ot_ref.dtype)


def _proj(x, g, w, gain, *, tm):
    B, S, D = x.shape
    width = gain.shape[0]
    heads_t = (w.shape[1] - width) // LANES
    tile = pl.BlockSpec((1, tm, D), lambda b, s: (b, s, 0))
    out_specs = [pl.BlockSpec((1, tm, width), lambda b, s: (b, s, 0))]
    out_shape = [jax.ShapeDtypeStruct((B, S, width), BF16)]
    if heads_t:
        rows = LANES + ONES_ROWS
        out_specs.append(pl.BlockSpec((1, heads_t, 1, rows, tm), lambda b, s: (b, 0, s, 0, 0)))
        out_shape.append(jax.ShapeDtypeStruct((B, heads_t, S // tm, rows, tm), BF16))
    return pl.pallas_call(
        functools.partial(_proj_kernel, n_normed=width),
        grid=(B, S // tm),
        in_specs=[tile, _resident((1, D)), _resident(w.shape), _resident((1, width))],
        out_specs=out_specs,
        out_shape=out_shape,
        compiler_params=_params("parallel", "parallel"),
        name="norm_proj",
    )(x, g[None, :], w.astype(BF16), gain[None, :])


def _attn_kernel(lam_ref, q_ref, k_ref, vt_ref, sub_ref, o_ref, m_ref, acc_ref, sa_ref, sb_ref, ta_ref, tb_ref,
                 *, lam_init):
    i = pl.program_id(2)
    tq, dv = o_ref.shape[1:]
    tk = vt_ref.shape[4]
    n = tq // tk
    assert n % 2 == 0
    q = q_ref[0]
    lane = lax.broadcasted_iota(jnp.int32, q.shape, 1)
    qs = (jnp.where(lane < DIFF_DH, q, jnp.zeros_like(q)), jnp.where(lane < DIFF_DH, jnp.zeros_like(q), q))
    m_ref[...] = jnp.full(m_ref.shape, -jnp.inf, F32)
    acc_ref[...] = jnp.zeros(acc_ref.shape, F32)

    def scores(j, dst, q0=0):
        dst_ref, top_ref = dst
        kb = k_ref[0, pl.ds(pl.multiple_of(j * tk, tk), tk), :]
        for c in range(2):
            sc = _dot_nt(kb, qs[c][q0:])
            dst_ref[c, :, q0:] = sc
            top_ref[c, :, q0:] = jnp.max(sc, axis=0, keepdims=True)

    def consume(src, j, q0=0, q1=tq, masked=False):
        src_ref, top_ref = src
        vt = vt_ref[0, 0, j]
        for c in range(2):
            sc = src_ref[c, :, q0:q1]
            if masked:
                key = lax.broadcasted_iota(jnp.int32, sc.shape, 0)
                qry = lax.broadcasted_iota(jnp.int32, sc.shape, 1)
                sc = jnp.where(key <= qry, sc, -jnp.inf)
                top = jnp.max(sc, axis=0, keepdims=True)
            else:
                top = top_ref[c, :, q0:q1]
            m_prev = m_ref[c, :, q0:q1]
            m_new = jnp.maximum(m_prev, top)
            p = jnp.exp2(sc - m_new)
            acc_ref[c, :, q0:q1] = jnp.exp2(m_prev - m_new) * acc_ref[c, :, q0:q1] + _dot(vt, p)
            m_ref[c, :, q0:q1] = m_new

    bufs = ((sa_ref, ta_ref), (sb_ref, tb_ref))
    scores(0, bufs[0])

    def two_blocks(pair, carry):
        j = 2 * pair
        scores(j + 1, bufs[1])
        consume(bufs[0], j)
        scores(j + 2, bufs[0])
        consume(bufs[1], j + 1)
        return carry

    lax.fori_loop(0, i * (n // 2), two_blocks, 0)
    for d in range(n):
        if d + 1 < n:
            scores(n * i + d + 1, bufs[(d + 1) % 2], (d + 1) * tk)
        consume(bufs[d % 2], n * i + d, d * tk, (d + 1) * tk, masked=True)
        if d + 1 < n:
            consume(bufs[d % 2], n * i + d, (d + 1) * tk, tq)

    lam = lam_ref[...]
    lam_full = (jnp.exp(jnp.sum(lam[0:1] * lam[1:2], axis=-1, keepdims=True))
                - jnp.exp(jnp.sum(lam[2:3] * lam[3:4], axis=-1, keepdims=True)) + lam_init)
    o = (acc_ref[0, :dv] / acc_ref[0, dv:dv + 1] - lam_full * (acc_ref[1, :dv] / acc_ref[1, dv:dv + 1]))
    ms = jnp.mean(o * o, axis=0, keepdims=True)
    o = o * lax.rsqrt(ms + EPS) * (sub_ref[...] * (1.0 - lam_init))
    o_ref[0] = o.T.astype(o_ref.dtype)


def _diff_attention(q, k, vt, lam, subln, lam_init):
    B, S, _ = q.shape
    H = DIFF_HEADS
    nk, rows, tk = vt.shape[2:]
    tq = ATTN_QUERY_BLOCKS * tk
    dv = subln.shape[0]
    k_spec = pl.BlockSpec((1, S, LANES), lambda b, h, i: (b, 0, h))
    vt_spec = pl.BlockSpec((1, 1, nk, rows, tk), lambda b, h, i: (b, h, 0, 0, 0))
    q_spec = pl.BlockSpec((1, tq, LANES), lambda b, h, i: (b, i, h))
    return pl.pallas_call(
        functools.partial(_attn_kernel, lam_init=lam_init),
        grid=(B, H, S // tq),
        in_specs=[_resident(lam.shape), q_spec, k_spec, vt_spec, _resident((dv, 1))],
        out_specs=q_spec,
        out_shape=jax.ShapeDtypeStruct(q.shape, BF16),
        scratch_shapes=[pltpu.VMEM((2, 1, tq), F32), pltpu.VMEM((2, rows, tq), F32),
                        pltpu.VMEM((2, tk, tq), F32), pltpu.VMEM((2, tk, tq), F32),
                        pltpu.VMEM((2, 1, tq), F32), pltpu.VMEM((2, 1, tq), F32)],
        compiler_params=_params("parallel", "parallel", "parallel"),
        name="diff_attention",
    )(lam, q, k, vt, subln[:, None])


def _out_proj_kernel(x_ref, o_ref, w_ref, y_ref):
    y_ref[0] = x_ref[0] + _dot(o_ref[0], w_ref[...])


def _out_proj(x, o, w, *, tm=2048):
    B, S, D = x.shape
    tile = lambda n: pl.BlockSpec((1, tm, n), lambda b, s: (b, s, 0))
    return pl.pallas_call(
        _out_proj_kernel,
        grid=(B, S // tm),
        in_specs=[tile(D), tile(o.shape[2]), _resident(w.shape)],
        out_specs=tile(D),
        out_shape=jax.ShapeDtypeStruct(x.shape, F32),
        compiler_params=_params("parallel", "parallel"),
        name="out_proj",
    )(x, o, w.astype(BF16))


def kernel(x, a_norm, a_w_in, a_conv_w, a_log, a_dt_bias, a_out_norm, a_w_out, kv_norm, kv_w, k_norm,
           b_norm, b_w_q, b_q_norm, b_lambda, b_subln, b_w_out, f_norm, f_w_up, f_conv_w, f_conv_b,
           f_w_down):
    n_a = a_norm.shape[0]
    depth = f_norm.shape[0]
    groups = DIFF_HEADS * 2
    a_w_in, a_w_out, kv_w, b_w_q, b_w_out, f_w_up, f_w_down = (
        w.astype(BF16) for w in (a_w_in, a_w_out, kv_w, b_w_q, b_w_out, f_w_up, f_w_down))
    k_sh = vt_sh = None
    for i in range(depth):
        if i < n_a:
            q, k, v, z, gcol, grow = _gdn_in(x, a_norm[i], a_w_in[i], a_conv_w[i], a_log[i], a_dt_bias[i])
            x = _gdn_chunk(x, q, k, v, z, gcol, grow, a_out_norm[i], a_w_out[i])
        else:
            if i == n_a:
                k_sh, vt_sh = _proj(x, kv_norm, kv_w, jnp.tile(k_norm, groups), tm=ATTN_BLOCK)
            j = i - n_a
            lam_init = 0.8 - 0.6 * math.exp(-0.3 * i)
            q_gain = jnp.tile(b_q_norm[j], groups) * (DIFF_DH ** -0.5 * LOG2E)
            (q,) = _proj(x, b_norm[j], b_w_q[j], q_gain, tm=4 * ATTN_BLOCK)
            o = _diff_attention(q, k_sh, vt_sh, b_lambda[j], b_subln[j], lam_init)
            x = _out_proj(x, o, b_w_out[j])
        x = _conv_ffn(x, f_norm[i], f_w_up[i], f_conv_w[i], f_conv_b[i], f_w_down[i])
    return x
```

```python
import functools
import math

import jax
import jax.numpy as jnp
from jax import lax
from jax.experimental import pallas as pl
from jax.experimental.pallas import tpu as pltpu

F32 = jnp.float32
BF16 = jnp.bfloat16

EPS = 1e-6
LANES = 128
MXU_COLS = 256
HALO = 16
GDN_HEADS = 8
GDN_DK = 128
GDN_CHUNK = 64
GDN_ROWS = 256
DIFF_HEADS = 8
DIFF_DH = 64
ATTN_BLOCK = 512
ATTN_QUERY_BLOCKS = 4
LOG2E = math.log2(math.e)
ONES_ROWS = 16
FF_COLS = MXU_COLS
FFN_AHEAD = 3
FFN_DOWN_GROUP = 4
SOLVE_BLOCK = 16
VMEM_LIMIT = 56 * 1024 * 1024


def _params(*sem):
    return pltpu.CompilerParams(dimension_semantics=sem, vmem_limit_bytes=VMEM_LIMIT)


def _resident(shape):
    zeros = (0,) * len(shape)
    return pl.BlockSpec(shape, lambda *_: zeros, pipeline_mode=pl.Buffered(1))


def _rms(x, g):
    ms = jnp.mean(x * x, axis=-1, keepdims=True)
    return x * lax.rsqrt(ms + EPS) * g


def _sigmoid(x):
    return 1.0 / (1.0 + jnp.exp(-x))


def _silu(x):
    half = 0.5 * x
    return half * (1.0 + jnp.tanh(half))


def _dot(a, b):
    return jnp.dot(a.astype(BF16), b.astype(BF16), preferred_element_type=F32)


def _dot_nt(a, b):
    return lax.dot_general(a.astype(BF16), b.astype(BF16), (((1,), (1,)), ((), ())),
                           preferred_element_type=F32)


def _dot_tn(a, b):
    return lax.dot_general(a.astype(BF16), b.astype(BF16), (((0,), (0,)), ((), ())),
                           preferred_element_type=F32)


def _normed_with_halo(x_ref, xh_ref, g_ref):
    g = g_ref[...]
    h = _rms(x_ref[0], g).astype(BF16)
    keep = (pl.program_id(1) > 0).astype(F32)
    hh = (_rms(xh_ref[0], g) * keep).astype(BF16)
    return h, jnp.concatenate([hh, h], axis=0)


def _causal_taps(rows, cw, tm, bias=None):
    taps = cw.shape[0]
    y = bias
    for d in range(taps):
        term = rows(HALO - d, HALO - d + tm) * cw[taps - 1 - d:taps - d]
        y = term if y is None else y + term
    return y


def _halo_specs(tm, d):
    tile = pl.BlockSpec((1, tm, d), lambda b, s: (b, s, 0))
    halo = pl.BlockSpec((1, HALO, d), lambda b, s: (b, jnp.maximum(s * (tm // HALO) - 1, 0), 0))
    return tile, halo


def _ffn_kernel(x_ref, g_ref, wup_ref, cw_ref, cb_ref, wd_ref, o_ref, u_ref, hist_ref, act_ref):
    h = _rms(x_ref[0], g_ref[...]).astype(BF16)
    tm = x_ref.shape[1]
    n_slots = u_ref.shape[0]
    ff = wd_ref.shape[0]
    n_slabs = ff // FF_COLS

    @pl.when(pl.program_id(1) == 0)
    def _():
        hist_ref[...] = jnp.zeros(hist_ref.shape, F32)

    def cols(c, part):
        return slice(part * ff + c * FF_COLS, part * ff + (c + 1) * FF_COLS)

    def up_proj(c):
        for part in range(2):
            u_ref[c % n_slots, part, :HALO] = hist_ref[c, part]
            u_ref[c % n_slots, part, HALO:] = _dot(h, wup_ref[:, cols(c, part)])

    def conv(c, part):
        y = _causal_taps(lambda a, b: u_ref[c % n_slots, part, a:b, :], cw_ref[:, cols(c, part)], tm,
                         cb_ref[:, cols(c, part)])
        hist_ref[c, part] = u_ref[c % n_slots, part, tm:, :]
        return y

    acc = x_ref[0]
    for c in range(min(FFN_AHEAD, n_slabs)):
        up_proj(c)
    for c in range(n_slabs):
        if c + FFN_AHEAD < n_slabs:
            up_proj(c + FFN_AHEAD)
        group, first = divmod(c, FFN_DOWN_GROUP)
        first *= FF_COLS
        act_ref[group % 2, :, first:first + FF_COLS] = (_silu(conv(c, 0)) * conv(c, 1)).astype(BF16)
        if (c + 1) % FFN_DOWN_GROUP == 0 or c + 1 == n_slabs:
            width = first + FF_COLS
            start = group * FFN_DOWN_GROUP * FF_COLS
            acc = acc + _dot(act_ref[group % 2, :, :width], wd_ref[start:start + width, :])
    o_ref[0] = acc


def _conv_ffn(x, g, w_up, conv_w, conv_b, w_down, *, tm=512):
    B, S, D = x.shape
    ff = w_down.shape[0]
    assert ff % FF_COLS == 0 and S % tm == 0
    tile = pl.BlockSpec((1, tm, D), lambda b, s: (b, s, 0))
    return pl.pallas_call(
        _ffn_kernel,
        grid=(B, S // tm),
        in_specs=[tile, _resident((1, D)), _resident(w_up.shape), _resident(conv_w.shape),
                  _resident((1, 2 * ff)), _resident(w_down.shape)],
        out_specs=tile,
        out_shape=jax.ShapeDtypeStruct(x.shape, F32),
        scratch_shapes=[pltpu.VMEM((FFN_AHEAD + 1, 2, HALO + tm, FF_COLS), F32),
                        pltpu.VMEM((ff // FF_COLS, 2, HALO, FF_COLS), F32),
                        pltpu.VMEM((2, tm, FFN_DOWN_GROUP * FF_COLS), BF16)],
        compiler_params=_params("parallel", "arbitrary"),
        name="conv_ffn",
    )(x, g[None, :], w_up.astype(BF16), conv_w, conv_b[None, :], w_down.astype(BF16))


def _chunk_cumsum(g):
    row = lax.broadcasted_iota(jnp.int32, g.shape, 0) & (GDN_CHUNK - 1)
    step = 1
    while step < GDN_CHUNK:
        g = g + jnp.where(row >= step, pltpu.roll(g, step, 0), 0.0)
        step *= 2
    return g


def _gdn_in_kernel(x_ref, xh_ref, g_ref, wqkv_ref, cw_ref, wz_ref, wab_ref, alog_ref, dtb_ref,
                   q_ref, k_ref, v_ref, z_ref, gcol_ref, grow_ref, u_ref):
    h, hext = _normed_with_halo(x_ref, xh_ref, g_ref)
    tm = x_ref.shape[1]
    qk_cols = GDN_HEADS * GDN_DK
    outs = (q_ref, k_ref, v_ref)
    n_chunks = wqkv_ref.shape[1] // MXU_COLS
    n_z = wz_ref.shape[1] // MXU_COLS
    assert n_z <= n_chunks and qk_cols % MXU_COLS == 0

    def tile_cols(c):
        return slice(c * MXU_COLS, (c + 1) * MXU_COLS)

    def project(c):
        u_ref[c % 2] = _dot(hext, wqkv_ref[:, tile_cols(c)])

    project(0)
    for c in range(n_chunks):
        cols = tile_cols(c)
        if c + 1 < n_chunks:
            project(c + 1)
        if (c + 1) * n_z // n_chunks > c * n_z // n_chunks:
            zc = tile_cols(c * n_z // n_chunks)
            z_ref[0, :, zc] = _dot(h, wz_ref[:, zc]).astype(z_ref.dtype)
        y = _causal_taps(lambda a, b: u_ref[c % 2, a:b, :], cw_ref[:, cols], tm)
        which, off = divmod(c * MXU_COLS, qk_cols)
        outs[which][0, :, off:off + MXU_COLS] = y.astype(outs[which].dtype)
    ab = _dot(h, wab_ref[...])
    t = ab + dtb_ref[...]
    softplus = jnp.maximum(t, 0.0) + jnp.log1p(jnp.exp(-jnp.abs(t)))
    cum = _chunk_cumsum(-jnp.exp(alog_ref[...]) * softplus)
    lane = lax.broadcasted_iota(jnp.int32, ab.shape, 1)
    gcol_ref[0] = jnp.where(lane < GDN_HEADS, cum, _sigmoid(ab))
    grow_ref[0] = cum.T[:GDN_HEADS, :]


def _gdn_in(x, g, w_in, conv_w, a_log, dt_bias, *, tm=512, act_dtype=BF16):
    B, S, D = x.shape
    H = GDN_HEADS
    qk = H * GDN_DK
    nqkv = conv_w.shape[1]
    nz = nqkv - 2 * qk
    wqkv = w_in[:, :nqkv].astype(BF16)
    wz = w_in[:, nqkv:nqkv + nz].astype(BF16)
    wab = jnp.pad(w_in[:, nqkv + nz:], ((0, 0), (0, LANES - 2 * H))).astype(BF16)
    alog = jnp.pad(a_log, (0, LANES - H))[None, :]
    dtb = jnp.pad(dt_bias, (0, LANES - H))[None, :]
    tile, halo = _halo_specs(tm, D)
    act = lambda n: jax.ShapeDtypeStruct((B, S, n), act_dtype)
    act_spec = lambda n: pl.BlockSpec((1, tm, n), lambda b, s: (b, s, 0))
    return pl.pallas_call(
        _gdn_in_kernel,
        grid=(B, S // tm),
        in_specs=[tile, halo, _resident((1, D)), _resident(wqkv.shape), _resident(conv_w.shape),
                  _resident(wz.shape), _resident(wab.shape), _resident(alog.shape), _resident(dtb.shape)],
        out_specs=[act_spec(qk), act_spec(qk), act_spec(nz), act_spec(nz), act_spec(LANES),
                   pl.BlockSpec((1, H, tm), lambda b, s: (b, 0, s))],
        out_shape=[act(qk), act(qk), act(nz), act(nz),
                   jax.ShapeDtypeStruct((B, S, LANES), F32), jax.ShapeDtypeStruct((B, H, S), F32)],
        scratch_shapes=[pltpu.VMEM((2, HALO + tm, MXU_COLS), F32)],
        compiler_params=_params("parallel", "parallel"),
        name="gdn_in",
    )(x, x, g[None, :], wqkv, conv_w, wz, wab, alog, dtb)


def _each(f, *lists):
    return [f(*args) for args in zip(*lists)]


def _same_block(ri, ci, size):
    shift = size.bit_length() - 1
    return (ri >> shift) == (ci >> shift)


def _unit_lower_inverses(a_s, ri, ci):
    eye = jnp.where(ri == ci, 1.0, 0.0)
    done = _each(lambda a: jnp.where(_same_block(ri, ci, SOLVE_BLOCK), a, 0.0), a_s)
    x = _each(lambda d: eye - d, done)
    power, order = done, 2
    while order < SOLVE_BLOCK:
        power = _each(_dot, power, power)
        x = _each(lambda x_, p: x_ + _dot(x_, p), x, power)
        order *= 2
    size = SOLVE_BLOCK
    while size < GDN_CHUNK:
        size *= 2
        part = a_s if size == GDN_CHUNK else _each(lambda a: jnp.where(_same_block(ri, ci, size), a, 0.0), a_s)
        t = _each(lambda p, d, x_: _dot(p - d, x_), part, done, x)
        x = _each(lambda x_, t_: x_ - _dot(x_, t_), x, t)
        done = part
    return x


def _gdn_chunk_kernel(q_ref, k_ref, v_ref, z_ref, gcol_ref, grow_ref, x_ref, onorm_ref, wout_ref,
                      o_ref, state_ref):
    R, C = GDN_ROWS, GDN_CHUNK
    ri = lax.broadcasted_iota(jnp.int32, (R, R), 0)
    ci = lax.broadcasted_iota(jnp.int32, (R, R), 1)
    same = _same_block(ri, ci, C)
    causal = same & (ri >= ci)
    strict = same & (ri > ci)
    gcol = gcol_ref[0]

    @pl.when(pl.program_id(1) == 0)
    def _():
        state_ref[...] = jnp.zeros(state_ref.shape, F32)

    heads = list(range(GDN_HEADS))
    head_lanes = [slice(h * LANES, (h + 1) * LANES) for h in heads]

    def l2_normed(t, scale):
        return t * (lax.rsqrt(jnp.sum(t * t, axis=-1, keepdims=True) + EPS) * scale)

    k = [l2_normed(_silu(k_ref[0, :, ln].astype(F32)), 1.0) for ln in head_lanes]
    kb = [k_.astype(BF16) for k_ in k]
    kk = _each(_dot_nt, kb, kb)
    gc = [gcol[:, h:h + 1] for h in heads]
    beta = [gcol[:, GDN_HEADS + h:GDN_HEADS + h + 1] for h in heads]
    decay = [jnp.exp(jnp.where(causal, g - grow_ref[0, h:h + 1, :], -jnp.inf)) for g, h in zip(gc, heads)]
    a = _each(lambda kk_, d, b: jnp.where(strict, kk_ * d * b, 0.0), kk, decay, beta)
    tinv = _unit_lower_inverses(a, ri, ci)
    q = [l2_normed(_silu(q_ref[0, :, ln].astype(F32)), GDN_DK ** -0.5) for ln in head_lanes]
    attn = _each(lambda q_, kb_, d: _dot_nt(q_, kb_) * d, q, kb, decay)
    v = [_silu(v_ref[0, :, ln].astype(F32)) for ln in head_lanes]
    egc = [jnp.exp(g) for g in gc]
    g_last = [jnp.concatenate([jnp.broadcast_to(g[c * C + C - 1:(c + 1) * C, :], (C, 1)) for c in range(R // C)],
                              axis=0) for g in gc]
    sol = _each(lambda t, v_, k_, b, e: _dot(t, jnp.concatenate([v_ * b, k_ * (b * e)], axis=1)),
                tinv, v, k, beta, egc)
    u = [s_[:, :LANES] for s_ in sol]
    w = [s_[:, LANES:] for s_ in sol]
    q_dec = _each(lambda q_, e: q_ * e, q, egc)
    k_dec = _each(lambda k_, gl, g: k_ * jnp.exp(gl - g), k, g_last, gc)

    state = [state_ref[h] for h in heads]
    corrected = [[] for _ in heads]
    from_state = [[] for _ in heads]
    for c in range(R // C):
        rows = slice(c * C, (c + 1) * C)
        r = _each(lambda w_, qd, st: _dot(jnp.concatenate([w_[rows], qd[rows]], axis=0), st), w, q_dec, state)
        uc = _each(lambda u_, r_: u_[rows] - r_[:C], u, r)
        for j in heads:
            corrected[j].append(uc[j])
            from_state[j].append(r[j][C:])
        state = _each(lambda st, gl, kd, uc_: st * jnp.exp(gl[c * C:c * C + 1, :]) + _dot_tn(kd[rows], uc_),
                      state, g_last, k_dec, uc)
    for h in heads:
        state_ref[h] = state[h]
    o = _each(lambda fs, at, co: jnp.concatenate(fs, axis=0) + _dot(at, jnp.concatenate(co, axis=0)),
              from_state, attn, corrected)
    o = _each(lambda o_, ln: _rms(o_, onorm_ref[...]) * _silu(z_ref[0, :, ln].astype(F32)), o, head_lanes)
    o_ref[0] = x_ref[0] + _dot(jnp.concatenate([o_.astype(BF16) for o_ in o], axis=1), wout_ref[...])


def _gdn_chunk(x, q, k, v, z, gcol, grow, out_norm, w_out):
    B, S, D = x.shape
    H, R = GDN_HEADS, GDN_ROWS
    dv = w_out.shape[0] // H
    row_block = lambda n: pl.BlockSpec((1, R, n), lambda b, s: (b, s, 0))
    return pl.pallas_call(
        _gdn_chunk_kernel,
        grid=(B, S // R),
        in_specs=[row_block(q.shape[2]), row_block(k.shape[2]), row_block(v.shape[2]), row_block(z.shape[2]),
                  row_block(LANES), pl.BlockSpec((1, H, R), lambda b, s: (b, 0, s)),
                  row_block(D), _resident((1, dv)), _resident(w_out.shape)],
        out_specs=row_block(D),
        out_shape=jax.ShapeDtypeStruct(x.shape, F32),
        scratch_shapes=[pltpu.VMEM((H, GDN_DK, dv), F32)],
        compiler_params=_params("parallel", "arbitrary"),
        name="gdn_chunk",
    )(q, k, v, z, gcol, grow, x, out_norm[None, :], w_out.astype(BF16))


def _proj_kernel(x_ref, g_ref, w_ref, gain_ref, o_ref, ot_ref=None, *, n_normed, g_cols):
    x = x_ref[0]
    xn = x * lax.rsqrt(jnp.mean(x * x, axis=-1, keepdims=True) + EPS)
    hs = [(xn * g_ref[r:r + 1, :]).astype(BF16) for r in range(g_ref.shape[0])]
    width = o_ref.shape[2]
    for c in range(w_ref.shape[1] // MXU_COLS):
        h = hs[min(c * MXU_COLS // g_cols, len(hs) - 1)]
        y = _dot(h, w_ref[:, c * MXU_COLS:(c + 1) * MXU_COLS])
        for half in range(2):
            col = c * MXU_COLS + half * LANES
            yy = y[:, half * LANES:(half + 1) * LANES]
            if col < n_normed:
                lane = lax.broadcasted_iota(jnp.int32, yy.shape, 1)
                low = lane < DIFF_DH
                sq = yy * yy
                ms = jnp.where(low,
                               jnp.sum(jnp.where(low, sq, 0.0), axis=-1, keepdims=True),
                               jnp.sum(jnp.where(low, 0.0, sq), axis=-1, keepdims=True)) * (1.0 / DIFF_DH)
                yy = yy * lax.rsqrt(ms + EPS) * gain_ref[:, col:col + LANES]
            if col < width:
                o_ref[0, :, col:col + LANES] = yy.astype(o_ref.dtype)
            else:
                head = (col - width) // LANES
                ot_ref[0, head, 0, :LANES, :] = yy.T.astype(ot_ref.dtype)
                ot_ref[0, head, 0, LANES:, :] = jnp.ones((ONES_ROWS, yy.shape[0]), ot_ref.dtype)


def _proj(x, g, w, gain, *, tm, g_cols=None):
    B, S, D = x.shape
    g = g.reshape(-1, D)
    g_cols = g_cols or w.shape[1]
    width = gain.shape[0]
    heads_t = (w.shape[1] - width) // LANES
    tile = pl.BlockSpec((1, tm, D), lambda b, s: (b, s, 0))
    out_specs = [pl.BlockSpec((1, tm, width), lambda b, s: (b, s, 0))]
    out_shape = [jax.ShapeDtypeStruct((B, S, width), BF16)]
    if heads_t:
        rows = LANES + ONES_ROWS
        out_specs.append(pl.BlockSpec((1, heads_t, 1, rows, tm), lambda b, s: (b, 0, s, 0, 0)))
        out_shape.append(jax.ShapeDtypeStruct((B, heads_t, S // tm, rows, tm), BF16))
    return pl.pallas_call(
        functools.partial(_proj_kernel, n_normed=width, g_cols=g_cols),
        grid=(B, S // tm),
        in_specs=[tile, _resident(g.shape), _resident(w.shape), _resident((1, width))],
        out_specs=out_specs,
        out_shape=out_shape,
        compiler_params=_params("parallel", "parallel"),
        name="norm_proj",
    )(x, g, w.astype(BF16), gain[None, :])


def _attn_kernel(lam_ref, q_ref, k_ref, vt_ref, sub_ref, o_ref, m_ref, acc_ref, sa_ref, sb_ref, ta_ref, tb_ref,
                 *, lam_init):
    i = pl.program_id(2)
    tq, dv = o_ref.shape[1:]
    tk = vt_ref.shape[4]
    n = tq // tk
    assert n % 2 == 0
    q = q_ref[0]
    lane = lax.broadcasted_iota(jnp.int32, q.shape, 1)
    qs = (jnp.where(lane < DIFF_DH, q, jnp.zeros_like(q)), jnp.where(lane < DIFF_DH, jnp.zeros_like(q), q))
    m_ref[...] = jnp.full(m_ref.shape, -jnp.inf, F32)
    acc_ref[...] = jnp.zeros(acc_ref.shape, F32)

    def scores(j, dst, q0=0):
        dst_ref, top_ref = dst
        kb = k_ref[0, pl.ds(pl.multiple_of(j * tk, tk), tk), :]
        for c in range(2):
            sc = _dot_nt(kb, qs[c][q0:])
            dst_ref[c, :, q0:] = sc
            top_ref[c, :, q0:] = jnp.max(sc, axis=0, keepdims=True)

    def consume(src, j, q0=0, q1=tq, masked=False):
        src_ref, top_ref = src
        vt = vt_ref[0, 0, j]
        for c in range(2):
            sc = src_ref[c, :, q0:q1]
            if masked:
                key = lax.broadcasted_iota(jnp.int32, sc.shape, 0)
                qry = lax.broadcasted_iota(jnp.int32, sc.shape, 1)
                sc = jnp.where(key <= qry, sc, -jnp.inf)
                top = jnp.max(sc, axis=0, keepdims=True)
            else:
                top = top_ref[c, :, q0:q1]
            m_prev = m_ref[c, :, q0:q1]
            m_new = jnp.maximum(m_prev, top)
            p = jnp.exp2(sc - m_new)
            acc_ref[c, :, q0:q1] = jnp.exp2(m_prev - m_new) * acc_ref[c, :, q0:q1] + _dot(vt, p)
            m_ref[c, :, q0:q1] = m_new

    bufs = ((sa_ref, ta_ref), (sb_ref, tb_ref))
    scores(0, bufs[0])

    def two_blocks(pair, carry):
        j = 2 * pair
        scores(j + 1, bufs[1])
        consume(bufs[0], j)
        scores(j + 2, bufs[0])
        consume(bufs[1], j + 1)
        return carry

    lax.fori_loop(0, i * (n // 2), two_blocks, 0)
    for d in range(n):
        if d + 1 < n:
            scores(n * i + d + 1, bufs[(d + 1) % 2], (d + 1) * tk)
        consume(bufs[d % 2], n * i + d, d * tk, (d + 1) * tk, masked=True)
        if d + 1 < n:
            consume(bufs[d % 2], n * i + d, (d + 1) * tk, tq)

    lam = lam_ref[...]
    lam_full = (jnp.exp(jnp.sum(lam[0:1] * lam[1:2], axis=-1, keepdims=True))
                - jnp.exp(jnp.sum(lam[2:3] * lam[3:4], axis=-1, keepdims=True)) + lam_init)
    o = (acc_ref[0, :dv] / acc_ref[0, dv:dv + 1] - lam_full * (acc_ref[1, :dv] / acc_ref[1, dv:dv + 1]))
    ms = jnp.mean(o * o, axis=0, keepdims=True)
    o = o * lax.rsqrt(ms + EPS) * (sub_ref[...] * (1.0 - lam_init))
    o_ref[0] = o.T.astype(o_ref.dtype)


def _diff_attention(q, k, vt, lam, subln, lam_init, *, q_tile0=0, k_tile0=0):
    B, S, _ = q.shape
    H = DIFF_HEADS
    nk, rows, tk = vt.shape[2:]
    tq = ATTN_QUERY_BLOCKS * tk
    dv = subln.shape[0]
    k_spec = pl.BlockSpec((1, S, LANES), lambda b, h, i: (b, 0, k_tile0 + h))
    vt_spec = pl.BlockSpec((1, 1, nk, rows, tk), lambda b, h, i: (b, h, 0, 0, 0))
    q_spec = pl.BlockSpec((1, tq, LANES), lambda b, h, i: (b, i, q_tile0 + h))
    o_spec = pl.BlockSpec((1, tq, dv), lambda b, h, i: (b, i, h))
    return pl.pallas_call(
        functools.partial(_attn_kernel, lam_init=lam_init),
        grid=(B, H, S // tq),
        in_specs=[_resident(lam.shape), q_spec, k_spec, vt_spec, _resident((dv, 1))],
        out_specs=o_spec,
        out_shape=jax.ShapeDtypeStruct((B, S, H * dv), BF16),
        scratch_shapes=[pltpu.VMEM((2, 1, tq), F32), pltpu.VMEM((2, rows, tq), F32),
                        pltpu.VMEM((2, tk, tq), F32), pltpu.VMEM((2, tk, tq), F32),
                        pltpu.VMEM((2, 1, tq), F32), pltpu.VMEM((2, 1, tq), F32)],
        compiler_params=_params("parallel", "parallel", "parallel"),
        name="diff_attention",
    )(lam, q, k, vt, subln[:, None])


def _out_proj_kernel(x_ref, o_ref, w_ref, y_ref):
    y_ref[0] = x_ref[0] + _dot(o_ref[0], w_ref[...])


def _out_proj(x, o, w, *, tm=1024):
    B, S, D = x.shape
    tile = lambda n: pl.BlockSpec((1, tm, n), lambda b, s: (b, s, 0))
    return pl.pallas_call(
        _out_proj_kernel,
        grid=(B, S // tm),
        in_specs=[tile(D), tile(o.shape[2]), _resident(w.shape)],
        out_specs=tile(D),
        out_shape=jax.ShapeDtypeStruct(x.shape, F32),
        compiler_params=_params("parallel", "parallel"),
        name="out_proj",
    )(x, o, w.astype(BF16))


def kernel(x, a_norm, a_w_in, a_conv_w, a_log, a_dt_bias, a_out_norm, a_w_out, kv_norm, kv_w, k_norm,
           b_norm, b_w_q, b_q_norm, b_lambda, b_subln, b_w_out, f_norm, f_w_up, f_conv_w, f_conv_b,
           f_w_down):
    n_a = a_norm.shape[0]
    depth = f_norm.shape[0]
    groups = DIFF_HEADS * 2
    a_w_in, a_w_out, kv_w, b_w_q, b_w_out, f_w_up, f_w_down = (
        w.astype(BF16) for w in (a_w_in, a_w_out, kv_w, b_w_q, b_w_out, f_w_up, f_w_down))
    qk_sh = vt_sh = None
    for i in range(depth):
        if i < n_a:
            q, k, v, z, gcol, grow = _gdn_in(x, a_norm[i], a_w_in[i], a_conv_w[i], a_log[i], a_dt_bias[i])
            x = _gdn_chunk(x, q, k, v, z, gcol, grow, a_out_norm[i], a_w_out[i])
        else:
            j = i - n_a
            lam_init = 0.8 - 0.6 * math.exp(-0.3 * i)
            q_gain = jnp.tile(b_q_norm[j], groups) * (DIFF_DH ** -0.5 * LOG2E)
            n_q = b_w_q.shape[2]
            if i == n_a:
                qk_sh, vt_sh = _proj(x, jnp.stack([b_norm[j], kv_norm]), jnp.concatenate([b_w_q[j], kv_w], axis=1),
                                     jnp.concatenate([q_gain, jnp.tile(k_norm, groups)]), tm=ATTN_BLOCK, g_cols=n_q)
                q, q_tile0 = qk_sh, 0
            else:
                (q,), q_tile0 = _proj(x, b_norm[j], b_w_q[j], q_gain, tm=2 * ATTN_BLOCK), 0
            o = _diff_attention(q, qk_sh, vt_sh, b_lambda[j], b_subln[j], lam_init,
                                q_tile0=q_tile0, k_tile0=n_q // LANES)
            x = _out_proj(x, o, b_w_out[j])
        x = _conv_ffn(x, f_norm[i], f_w_up[i], f_conv_w[i], f_conv_b[i], f_w_down[i])
    return x
```
